```python
import math
import jax, jax.numpy as jnp
from jax import lax
import numpy as np

D_MODEL = 1024
BATCH = 8
SEQ = 4096
DEPTH = 1
DEC_BATCH = 32
DEC_SEQ = 2048
PAST_LEN = 128

ATTN_WIDTH = D_MODEL // 2
CONV_WIDTH = D_MODEL - ATTN_WIDTH
QK_HEAD_DIM = 64
V_HEAD_DIM = 2 * QK_HEAD_DIM
N_DIFF_HEADS = ATTN_WIDTH // V_HEAD_DIM
Q_WIDTH = N_DIFF_HEADS * 2 * QK_HEAD_DIM
V_WIDTH = N_DIFF_HEADS * V_HEAD_DIM
IN_WIDTH = 2 * Q_WIDTH + V_WIDTH + 3 * CONV_WIDTH
SPLITS = (Q_WIDTH, 2 * Q_WIDTH, 2 * Q_WIDTH + V_WIDTH,
          2 * Q_WIDTH + V_WIDTH + CONV_WIDTH, 2 * Q_WIDTH + V_WIDTH + 2 * CONV_WIDTH)
CONV_K = 3
ROPE_THETA = 10000.0
Q_BLOCK = 128
N_GROUPS = 4
EXPERTS_PER_GROUP = 8
N_EXPERTS = N_GROUPS * EXPERTS_PER_GROUP
TOP_K_INNER = 2
D_FF_EXPERT = D_MODEL // 2
MOE_BLOCK = 128
LN_EPS = 1e-5
RMS_EPS = 1e-5
DEEPNORM_ALPHA = (2.0 * DEPTH) ** 0.25
DEEPNORM_BETA = (8.0 * DEPTH) ** -0.25

kernel_name = "hymba_style_diffattn_shortconv_hmoe_encoder"


def _layernorm(x):
    xf = x.astype(jnp.float32)
    mu = jnp.mean(xf, axis=-1, keepdims=True)
    var = jnp.mean(jnp.square(xf - mu), axis=-1, keepdims=True)
    return ((xf - mu) * lax.rsqrt(var + LN_EPS)).astype(x.dtype)


def _lambda_init(layer):
    return 0.8 - 0.6 * math.exp(-0.3 * layer)


def _rope_tables(seq_len, dim):
    inv = 1.0 / (ROPE_THETA ** (jnp.arange(0, dim, 2, dtype=jnp.float32) / dim))
    ang = jnp.arange(seq_len, dtype=jnp.float32)[:, None] * inv[None, :]
    return jnp.cos(ang), jnp.sin(ang)


def _apply_rope(x, cos, sin):
    half = cos.shape[1]
    shape = (1, cos.shape[0]) + (1,) * (x.ndim - 3) + (half,)
    c = cos.reshape(shape).astype(x.dtype)
    s = sin.reshape(shape).astype(x.dtype)
    x1, x2 = x[..., :half], x[..., half:]
    return jnp.concatenate([x1 * c - x2 * s, x1 * s + x2 * c], axis=-1)


def _diff_attention(q, k, v, lam, subln_g, lambda_init):
    b, s = q.shape[0], q.shape[1]
    nb = s // Q_BLOCK
    scale = QK_HEAD_DIM ** -0.5
    qb = q.reshape(b, nb, Q_BLOCK, N_DIFF_HEADS, 2, QK_HEAD_DIM).transpose(1, 0, 2, 3, 4, 5)

    def block(q_blk):
        sc = jnp.einsum('bqhcd,bkhcd->bhcqk', q_blk, k,
                        preferred_element_type=jnp.float32) * scale
        p = jax.nn.softmax(sc, axis=-1)
        w = p[:, :, 0] - lam * p[:, :, 1]
        return jnp.einsum('bhqk,bkhd->bqhd', w.astype(v.dtype), v)

    o = lax.map(block, qb)
    o = o.transpose(1, 0, 2, 3, 4).reshape(b, s, N_DIFF_HEADS, V_HEAD_DIM)
    of = o.astype(jnp.float32)
    of = of * lax.rsqrt(jnp.mean(jnp.square(of), axis=-1, keepdims=True) + RMS_EPS)
    of = of * subln_g.astype(jnp.float32) * (1.0 - lambda_init)
    return of.reshape(b, s, V_WIDTH).astype(v.dtype)


def _short_conv(gate_b, gate_c, h, w, bias):
    u = gate_c * h
    pad = CONV_K // 2
    s = u.shape[1]
    up = jnp.pad(u, ((0, 0), (pad, CONV_K - 1 - pad), (0, 0)))
    y = bias
    for j in range(CONV_K):
        y = y + up[:, j:j + s] * w[j]
    return gate_b * y


def _hier_moe(h, wg_r, bg_r, we_r, be_r, w_gate, w_up, w_down):
    b, s, d = h.shape
    t = b * s
    x = h.reshape(t, d)
    lg = (x @ wg_r).astype(jnp.float32) + bg_r.astype(jnp.float32)
    pg = jax.nn.softmax(lg, axis=-1)
    g_sel = jnp.argmax(lg, axis=-1)
    tok = jnp.arange(t)
    pg_sel = pg[tok, g_sel]
    le = (x @ we_r).astype(jnp.float32) + be_r.astype(jnp.float32)
    le_sel = le.reshape(t, N_GROUPS, EXPERTS_PER_GROUP)[tok, g_sel]
    top_l, top_i = lax.top_k(le_sel, TOP_K_INNER)
    top_p = jax.nn.softmax(top_l, axis=-1)
    weights = pg_sel[:, None] * top_p
    experts = g_sel[:, None].astype(jnp.int32) * EXPERTS_PER_GROUP + top_i.astype(jnp.int32)
    a = t * TOP_K_INNER
    e_flat = experts.reshape(a)
    w_flat = weights.reshape(a)
    tok_flat = jnp.repeat(jnp.arange(t, dtype=jnp.int32), TOP_K_INNER)
    order = jnp.argsort(e_flat)
    e_sorted = e_flat[order]
    counts = jnp.bincount(e_flat, length=N_EXPERTS).astype(jnp.int32)
    padded = ((counts + MOE_BLOCK - 1) // MOE_BLOCK) * MOE_BLOCK
    start = jnp.cumsum(counts) - counts
    pend = jnp.cumsum(padded)
    pstart = pend - padded
    dest = pstart[e_sorted] + (jnp.arange(a, dtype=jnp.int32) - start[e_sorted])
    nblk = -(-a // MOE_BLOCK) + N_EXPERTS
    rows = nblk * MOE_BLOCK
    row_tok = jnp.zeros((rows,), jnp.int32).at[dest].set(tok_flat[order])
    row_w = jnp.zeros((rows,), h.dtype).at[dest].set(w_flat[order].astype(h.dtype))
    blk_start = jnp.arange(nblk, dtype=jnp.int32) * MOE_BLOCK
    blk_exp = jnp.minimum(jnp.searchsorted(pend, blk_start, side='right'), N_EXPERTS - 1)
    xb = x[row_tok].reshape(nblk, MOE_BLOCK, d)

    def run(args):
        x_blk, e = args
        g = x_blk @ w_gate[e]
        u = x_blk @ w_up[e]
        return (jax.nn.silu(g) * u) @ w_down[e]

    yb = lax.map(run, (xb, blk_exp))
    y_rows = yb.reshape(rows, d) * row_w[:, None]
    out = jnp.zeros((t, d), h.dtype).at[row_tok].add(y_rows)
    return out.reshape(b, s, d)


def _encoder(x, c, w_ada, b_ada, w_in, lambda_q1, lambda_k1, lambda_q2, lambda_k2,
             attn_subln_g, conv_w, conv_b, w_out, ln1_g, ln1_b,
             router_group_w, router_group_b, router_expert_w, router_expert_b,
             expert_w_gate, expert_w_up, expert_w_down, ln2_g, ln2_b):
    b, s, _ = x.shape
    cos, sin = _rope_tables(s, QK_HEAD_DIM)
    for l in range(DEPTH):
        lam_init = _lambda_init(l)
        ada = jax.nn.silu(c) @ w_ada[l] + b_ada[l]
        sh_m, sc_m, g_m, sh_f, sc_f, g_f = jnp.split(ada[:, None, :], 6, axis=-1)
        h = _layernorm(x) * (1.0 + sc_m) + sh_m
        proj = h @ w_in[l]
        q, k, v, cb, cc, ch = jnp.split(proj, SPLITS, axis=-1)
        q = _apply_rope(q.reshape(b, s, N_DIFF_HEADS, 2, QK_HEAD_DIM), cos, sin)
        k = _apply_rope(k.reshape(b, s, N_DIFF_HEADS, 2, QK_HEAD_DIM), cos, sin)
        v = v.reshape(b, s, N_DIFF_HEADS, V_HEAD_DIM)
        lam = (jnp.exp(jnp.sum(lambda_q1[l].astype(jnp.float32) * lambda_k1[l].astype(jnp.float32)))
               - jnp.exp(jnp.sum(lambda_q2[l].astype(jnp.float32) * lambda_k2[l].astype(jnp.float32)))
               + lam_init)
        attn = _diff_attention(q, k, v, lam, attn_subln_g[l], lam_init)
        conv = _short_conv(cb, cc, ch, conv_w[l], conv_b[l])
        mix = jnp.concatenate([attn, conv], axis=-1) @ w_out[l]
        x = _layernorm(DEEPNORM_ALPHA * x + g_m * mix) * ln1_g[l] + ln1_b[l]
        h2 = _layernorm(x) * (1.0 + sc_f) + sh_f
        f = _hier_moe(h2, router_group_w[l], router_group_b[l], router_expert_w[l],
                      router_expert_b[l], expert_w_gate[l], expert_w_up[l], expert_w_down[l])
        x = _layernorm(DEEPNORM_ALPHA * x + g_f * f) * ln2_g[l] + ln2_b[l]
    return x


def setup_inputs(seed: int = 0) -> dict:
    key = jax.random.key(seed)
    ks = jax.random.split(key, 26)
    f32 = jnp.float32
    D = D_MODEL
    nrm = lambda k, shape, sc: jax.random.normal(k, shape, f32) * sc
    col_scale = jnp.concatenate([
        jnp.ones((2 * Q_WIDTH,), f32), jnp.full((V_WIDTH,), DEEPNORM_BETA, f32),
        jnp.ones((2 * CONV_WIDTH,), f32), jnp.full((CONV_WIDTH,), DEEPNORM_BETA, f32)])
    return {
        "x_prompt": nrm(ks[0], (BATCH, SEQ, D), 1.0),
        "x_sample": nrm(ks[1], (DEC_BATCH, DEC_SEQ, D), 1.0),
        "c_prompt": nrm(ks[2], (BATCH, D), 1.0),
        "c_sample": nrm(ks[3], (DEC_BATCH, D), 1.0),
        "w_ada": nrm(ks[4], (DEPTH, D, 6 * D), 0.5 * D ** -0.5),
        "b_ada": nrm(ks[5], (DEPTH, 6 * D), 0.01),
        "w_in": nrm(ks[6], (DEPTH, D, IN_WIDTH), D ** -0.5) * col_scale,
        "lambda_q1": nrm(ks[7], (DEPTH, QK_HEAD_DIM), 0.1),
        "lambda_k1": nrm(ks[8], (DEPTH, QK_HEAD_DIM), 0.1),
        "lambda_q2": nrm(ks[9], (DEPTH, QK_HEAD_DIM), 0.1),
        "lambda_k2": nrm(ks[10], (DEPTH, QK_HEAD_DIM), 0.1),
        "attn_subln_g": 1.0 + nrm(ks[11], (DEPTH, V_HEAD_DIM), 0.02),
        "conv_w": nrm(ks[12], (DEPTH, CONV_K, CONV_WIDTH), CONV_K ** -0.5),
        "conv_b": nrm(ks[13], (DEPTH, CONV_WIDTH), 0.01),
        "w_out": nrm(ks[14], (DEPTH, D, D), D ** -0.5 * DEEPNORM_BETA),
        "ln1_g": 1.0 + nrm(ks[15], (DEPTH, D), 0.02),
        "ln1_b": nrm(ks[16], (DEPTH, D), 0.02),
        "router_group_w": nrm(ks[17], (DEPTH, D, N_GROUPS), D ** -0.5),
        "router_group_b": nrm(ks[18], (DEPTH, N_GROUPS), 0.01),
        "router_expert_w": nrm(ks[19], (DEPTH, D, N_EXPERTS), D ** -0.5),
        "router_expert_b": nrm(ks[20], (DEPTH, N_EXPERTS), 0.01),
        "expert_w_gate": nrm(ks[21], (DEPTH, N_EXPERTS, D, D_FF_EXPERT), D ** -0.5 * DEEPNORM_BETA),
        "expert_w_up": nrm(ks[22], (DEPTH, N_EXPERTS, D, D_FF_EXPERT), D ** -0.5 * DEEPNORM_BETA),
        "expert_w_down": nrm(ks[23], (DEPTH, N_EXPERTS, D_FF_EXPERT, D), D_FF_EXPERT ** -0.5 * DEEPNORM_BETA),
        "ln2_g": 1.0 + nrm(ks[24], (DEPTH, D), 0.02),
        "ln2_b": nrm(ks[25], (DEPTH, D), 0.02),
    }


def reference(x_prompt, x_sample, c_prompt, c_sample, w_ada, b_ada, w_in,
              lambda_q1, lambda_k1, lambda_q2, lambda_k2, attn_subln_g, conv_w, conv_b,
              w_out, ln1_g, ln1_b, router_group_w, router_group_b, router_expert_w,
              router_expert_b, expert_w_gate, expert_w_up, expert_w_down, ln2_g, ln2_b):
    y_prompt = _encoder(x_prompt, c_prompt, w_ada, b_ada, w_in, lambda_q1, lambda_k1,
                        lambda_q2, lambda_k2, attn_subln_g, conv_w, conv_b, w_out, ln1_g, ln1_b,
                        router_group_w, router_group_b, router_expert_w, router_expert_b,
                        expert_w_gate, expert_w_up, expert_w_down, ln2_g, ln2_b)
    y_sample = _encoder(x_sample, c_sample, w_ada, b_ada, w_in, lambda_q1, lambda_k1,
                        lambda_q2, lambda_k2, attn_subln_g, conv_w, conv_b, w_out, ln1_g, ln1_b,
                        router_group_w, router_group_b, router_expert_w, router_expert_b,
                        expert_w_gate, expert_w_up, expert_w_down, ln2_g, ln2_b)
    return (y_prompt, y_sample)
```

```python
import functools
import math

import jax
import jax.numpy as jnp
from jax import lax
from jax.experimental import pallas as pl
from jax.experimental.pallas import tpu as pltpu

D_MODEL = 1024
ATTN_WIDTH = 512
CONV_WIDTH = 512
QK_HEAD_DIM = 64
V_HEAD_DIM = 128
N_DIFF_HEADS = 4
IN_WIDTH = 3072
CONV_K = 3
ROPE_THETA = 10000.0
N_GROUPS = 4
EXPERTS_PER_GROUP = 8
N_EXPERTS = 32
D_FF_EXPERT = 512
LN_EPS = 1e-5
RMS_EPS = 1e-5
DEPTH = 1
DEEPNORM_ALPHA = (2.0 * DEPTH) ** 0.25
LAMBDA_INIT = 0.8 - 0.6 * math.exp(-0.3 * 0)

LANES = 128
BF16_SUBLANES = 16
VMEM_LIMIT = 48 * 1024 * 1024

TM_PROJ = 512
TQ = 512
TK = 512
TM_MIX = 256
TM_ROW = 512
EXPERT_BLOCK = 256
ROUTER_LANES = 128
EXPERT_LANE0 = N_GROUPS

BF16 = jnp.bfloat16
F32 = jnp.float32


def _cparams(sem):
    return pltpu.CompilerParams(dimension_semantics=sem, vmem_limit_bytes=VMEM_LIMIT)


def _layernorm(x):
    mu = jnp.mean(x, axis=-1, keepdims=True)
    xc = x - mu
    var = jnp.mean(xc * xc, axis=-1, keepdims=True)
    return xc * lax.rsqrt(var + LN_EPS)


def _split_hi_lo(a):
    hi = a.astype(BF16)
    lo = (a - hi.astype(F32)).astype(BF16)
    return hi, lo


def _dot(a, b):
    return jnp.dot(a, b, preferred_element_type=F32)


def _ada_kernel(c_ref, whi_ref, wlo_ref, b_ref, o_ref):
    c = c_ref[...]
    s = c * jax.nn.sigmoid(c)
    s_hi, s_lo = _split_hi_lo(s)
    acc = _dot(s_hi, whi_ref[...]) + _dot(s_lo, whi_ref[...]) + _dot(s_hi, wlo_ref[...])
    o_ref[...] = acc + b_ref[...]


def _ada(c, w_hi, w_lo, b):
    bsz = c.shape[0]
    n = w_hi.shape[1]
    tn = 1024
    return pl.pallas_call(
        _ada_kernel,
        out_shape=jax.ShapeDtypeStruct((bsz, n), F32),
        grid=(n // tn,),
        in_specs=[
            pl.BlockSpec((bsz, D_MODEL), lambda j: (0, 0)),
            pl.BlockSpec((D_MODEL, tn), lambda j: (0, j)),
            pl.BlockSpec((D_MODEL, tn), lambda j: (0, j)),
            pl.BlockSpec((1, tn), lambda j: (0, j)),
        ],
        out_specs=pl.BlockSpec((bsz, tn), lambda j: (0, j)),
        compiler_params=_cparams(("arbitrary",)),
        name="ada",
    )(c, w_hi, w_lo, b)


def _rope(x, cos_t, sin_t):
    lane = lax.broadcasted_iota(jnp.int32, x.shape, 1)
    upper = (lane & 32) != 0
    partner = jnp.where(upper, pltpu.roll(x, 32, axis=1), pltpu.roll(x, LANES - 32, axis=1))
    return x * cos_t + partner * sin_t


def _in_proj_kernel(x_ref, ada_ref, w_ref, cos_ref, sin_ref,
                    q_ref, k_ref, v_ref, cb_ref, u_ref):
    x = x_ref[...]
    ada = ada_ref[0]
    sh_m = ada[0:1, :]
    sc_m = ada[1:2, :]
    h = (_layernorm(x) * (1.0 + sc_m) + sh_m).astype(BF16)
    cos_t = cos_ref[...]
    sin_t = sin_ref[...]
    qk_scale = QK_HEAD_DIM ** -0.5
    for j in range(4):
        lo = j * LANES
        qj = _dot(h, w_ref[:, lo:lo + LANES])
        q_ref[:, lo:lo + LANES] = (_rope(qj, cos_t, sin_t) * qk_scale).astype(BF16)
        kj = _dot(h, w_ref[:, 512 + lo:512 + lo + LANES])
        k_ref[:, lo:lo + LANES] = _rope(kj, cos_t, sin_t).astype(BF16)
    v_ref[...] = _dot(h, w_ref[:, 1024:1536]).astype(BF16)
    cb_ref[...] = _dot(h, w_ref[:, 1536:2048]).astype(BF16)
    cc = _dot(h, w_ref[:, 2048:2560])
    ch = _dot(h, w_ref[:, 2560:3072])
    u_ref[...] = (cc * ch).astype(BF16)


def _in_proj(x2d, ada3, w_in_bf, cos_t, sin_t, seq):
    t = x2d.shape[0]
    tm = TM_PROJ
    tiles_per_seq = seq // tm
    out = jax.ShapeDtypeStruct((t, 512), BF16)
    ospec = pl.BlockSpec((tm, 512), lambda i: (i, 0))
    return pl.pallas_call(
        _in_proj_kernel,
        out_shape=(out,) * 5,
        grid=(t // tm,),
        in_specs=[
            pl.BlockSpec((tm, D_MODEL), lambda i: (i, 0)),
            pl.BlockSpec((1, 6, D_MODEL), lambda i: (i // tiles_per_seq, 0, 0)),
            pl.BlockSpec((D_MODEL, IN_WIDTH), lambda i: (0, 0)),
            pl.BlockSpec((tm, LANES), lambda i: (i % tiles_per_seq, 0)),
            pl.BlockSpec((tm, LANES), lambda i: (i % tiles_per_seq, 0)),
        ],
        out_specs=(ospec,) * 5,
        compiler_params=_cparams(("arbitrary",)),
        name="in_proj",
    )(x2d, ada3, w_in_bf, cos_t, sin_t)


def _diff_attn_kernel(q_ref, k_ref, v_ref, lq1_ref, lk1_ref, lq2_ref, lk2_ref, g_ref, o_ref, *, seq):
    q = q_ref[0]
    lane = lax.broadcasted_iota(jnp.int32, q.shape, 1)
    zero = jnp.zeros_like(q)
    qq = jnp.concatenate([jnp.where(lane < QK_HEAD_DIM, q, zero),
                          jnp.where(lane >= QK_HEAD_DIM, q, zero)], axis=0)
    nk = seq // TK

    def body(j, carry):
        m, l, acc = carry
        start = pl.multiple_of(j * TK, TK)
        kc = k_ref[0, pl.ds(start, TK), :]
        vc = v_ref[0, pl.ds(start, TK), :]
        s = lax.dot_general(qq, kc, (((1,), (1,)), ((), ())), preferred_element_type=F32)
        m_new = jnp.maximum(m, jnp.max(s, axis=-1, keepdims=True))
        alpha = jnp.exp(m - m_new)
        p = jnp.exp(s - m_new)
        l = alpha * l + jnp.sum(p, axis=-1, keepdims=True)
        acc = alpha * acc + _dot(p.astype(BF16), vc)
        return m_new, l, acc

    m0 = jnp.full((2 * TQ, 1), -jnp.inf, F32)
    l0 = jnp.zeros((2 * TQ, 1), F32)
    a0 = jnp.zeros((2 * TQ, V_HEAD_DIM), F32)
    m, l, acc = lax.fori_loop(0, nk, body, (m0, l0, a0))
    o = acc / l
    lam = (jnp.exp(jnp.sum(lq1_ref[...] * lk1_ref[...], axis=-1, keepdims=True))
           - jnp.exp(jnp.sum(lq2_ref[...] * lk2_ref[...], axis=-1, keepdims=True)) + LAMBDA_INIT)
    of = o[:TQ] - lam * o[TQ:]
    of = of * lax.rsqrt(jnp.mean(of * of, axis=-1, keepdims=True) + RMS_EPS)
    of = of * g_ref[...] * (1.0 - LAMBDA_INIT)
    o_ref[0] = of.astype(BF16)


def _diff_attn(q, k, v, lq1, lk1, lq2, lk2, g):
    b, seq, _ = q.shape
    lam_spec = pl.BlockSpec((1, QK_HEAD_DIM), lambda bi, h, qi: (0, 0))
    return pl.pallas_call(
        functools.partial(_diff_attn_kernel, seq=seq),
        out_shape=jax.ShapeDtypeStruct((b, seq, ATTN_WIDTH), BF16),
        grid=(b, N_DIFF_HEADS, seq // TQ),
        in_specs=[
            pl.BlockSpec((1, TQ, LANES), lambda bi, h, qi: (bi, qi, h)),
            pl.BlockSpec((1, seq, LANES), lambda bi, h, qi: (bi, 0, h)),
            pl.BlockSpec((1, seq, LANES), lambda bi, h, qi: (bi, 0, h)),
            lam_spec, lam_spec, lam_spec, lam_spec,
            pl.BlockSpec((1, V_HEAD_DIM), lambda bi, h, qi: (0, 0)),
        ],
        out_specs=pl.BlockSpec((1, TQ, LANES), lambda bi, h, qi: (bi, qi, h)),
        compiler_params=_cparams(("arbitrary", "arbitrary", "arbitrary")),
        name="diff_attn",
    )(q, k, v, lq1, lk1, lq2, lk2, g)


def _lane_min_index(mask, lane_f):
    return jnp.min(jnp.where(mask, lane_f, float(ROUTER_LANES)), axis=-1, keepdims=True)


def _mix_route_kernel(x_ref, attn_ref, cb_ref, u_ref, uprev_ref, unext_ref, ada_ref,
                      cw_ref, cbias_ref, woa_ref, woc_ref, g1_ref, b1_ref, wr_ref, br_ref,
                      x1_ref, h2_ref, rt_ref, cnt_ref, base_ref, *, seq):
    i = pl.program_id(0)
    tm = TM_MIX

    @pl.when(i == 0)
    def _():
        base_ref[...] = jnp.zeros_like(base_ref)

    ada = ada_ref[0]
    g_m = ada[2:3, :]
    sh_f = ada[3:4, :]
    sc_f = ada[4:5, :]

    u = u_ref[...].astype(F32)
    row = lax.broadcasted_iota(jnp.int32, u.shape, 0)
    not_seq_start = ((i * tm) % seq != 0).astype(F32)
    not_seq_end = (((i + 1) * tm) % seq != 0).astype(F32)
    halo_prev = uprev_ref[...].astype(F32)[BF16_SUBLANES - 1:BF16_SUBLANES, :] * not_seq_start
    halo_next = unext_ref[...].astype(F32)[0:1, :] * not_seq_end
    u_prev = jnp.where(row == 0, halo_prev, pltpu.roll(u, 1, axis=0))
    u_next = jnp.where(row == tm - 1, halo_next, pltpu.roll(u, tm - 1, axis=0))
    cw = cw_ref[...]
    y = cbias_ref[...] + u_prev * cw[0:1, :]
    y = y + u * cw[1:2, :]
    y = y + u_next * cw[2:3, :]
    conv = (cb_ref[...].astype(F32) * y).astype(BF16)

    mix = _dot(attn_ref[...], woa_ref[...]) + _dot(conv, woc_ref[...])
    x1 = _layernorm(DEEPNORM_ALPHA * x_ref[...] + g_m * mix) * g1_ref[...] + b1_ref[...]
    x1_ref[...] = x1
    h2 = _layernorm(x1) * (1.0 + sc_f) + sh_f
    h2_ref[...] = h2

    h_hi, h_lo = _split_hi_lo(h2)
    both = _dot(h_hi, wr_ref[...])
    logits = (both[:, :ROUTER_LANES] + both[:, ROUTER_LANES:]
              + _dot(h_lo, wr_ref[:, :ROUTER_LANES]) + br_ref[...])

    lane = lax.broadcasted_iota(jnp.int32, logits.shape, 1)
    lane_f = lane.astype(F32)
    neg = -jnp.inf
    is_group = lane < N_GROUPS
    lg = jnp.where(is_group, logits, neg)
    lg_max = jnp.max(lg, axis=-1, keepdims=True)
    g_sel = _lane_min_index(lg == lg_max, lane_f)
    pg_sel = 1.0 / jnp.sum(jnp.exp(lg - lg_max), axis=-1, keepdims=True)

    first = EXPERT_LANE0 + EXPERTS_PER_GROUP * g_sel
    in_group = (lane_f >= first) & (lane_f < first + EXPERTS_PER_GROUP)
    le = jnp.where(in_group, logits, neg)
    l0 = jnp.max(le, axis=-1, keepdims=True)
    i0 = _lane_min_index(le == l0, lane_f)
    le2 = jnp.where(lane_f == i0, neg, le)
    l1 = jnp.max(le2, axis=-1, keepdims=True)
    i1 = _lane_min_index(le2 == l1, lane_f)
    t_exp = jnp.exp(l1 - l0)
    p0 = 1.0 / (1.0 + t_exp)
    w0 = pg_sel * p0
    w1 = pg_sel * (t_exp * p0)

    oh0 = lane_f == i0
    oh1 = lane_f == i1
    onehots = jnp.concatenate([jnp.where(oh0, 1.0, 0.0), jnp.where(oh1, 1.0, 0.0)], axis=1).astype(BF16)
    r_i = lax.broadcasted_iota(jnp.int32, (tm, tm), 0)
    c_i = lax.broadcasted_iota(jnp.int32, (tm, tm), 1)
    tri = jnp.where(c_i < r_i, 1.0, 0.0).astype(BF16)
    before = _dot(tri, onehots)
    cnt0 = jnp.sum(jnp.where(oh0, 1.0, 0.0), axis=0, keepdims=True)
    cnt1 = jnp.sum(jnp.where(oh1, 1.0, 0.0), axis=0, keepdims=True)
    base = base_ref[...]
    rank0 = jnp.sum(jnp.where(oh0, before[:, :ROUTER_LANES] + base, 0.0), axis=-1, keepdims=True)
    rank1 = jnp.sum(jnp.where(oh1, before[:, ROUTER_LANES:] + base + cnt0, 0.0), axis=-1, keepdims=True)
    new_base = base + cnt0 + cnt1
    base_ref[...] = new_base
    cnt_ref[...] = new_base

    rt = jnp.where(lane == 0, i0 - EXPERT_LANE0, 0.0)
    rt = jnp.where(lane == 1, i1 - EXPERT_LANE0, rt)
    rt = jnp.where(lane == 2, rank0, rt)
    rt = jnp.where(lane == 3, rank1, rt)
    rt = jnp.where(lane == 4, w0, rt)
    rt = jnp.where(lane == 5, w1, rt)
    rt_ref[...] = rt


def _mix_route(x2d, attn2d, cb, u, ada3, conv_w, conv_b, woa, woc, g1, b1, wr, br, seq):
    t = x2d.shape[0]
    tm = TM_MIX
    tiles_per_seq = seq // tm
    hb = tm // BF16_SUBLANES
    n_halo = t // BF16_SUBLANES
    const = lambda i: (0, 0)
    return pl.pallas_call(
        functools.partial(_mix_route_kernel, seq=seq),
        out_shape=(
            jax.ShapeDtypeStruct((t, D_MODEL), F32),
            jax.ShapeDtypeStruct((t, D_MODEL), F32),
            jax.ShapeDtypeStruct((t, ROUTER_LANES), F32),
            jax.ShapeDtypeStruct((1, ROUTER_LANES), F32),
        ),
        grid=(t // tm,),
        in_specs=[
            pl.BlockSpec((tm, D_MODEL), lambda i: (i, 0)),
            pl.BlockSpec((tm, ATTN_WIDTH), lambda i: (i, 0)),
            pl.BlockSpec((tm, CONV_WIDTH), lambda i: (i, 0)),
            pl.BlockSpec((tm, CONV_WIDTH), lambda i: (i, 0)),
            pl.BlockSpec((BF16_SUBLANES, CONV_WIDTH), lambda i: (jnp.maximum(i * hb - 1, 0), 0)),
            pl.BlockSpec((BF16_SUBLANES, CONV_WIDTH), lambda i: (jnp.minimum((i + 1) * hb, n_halo - 1), 0)),
            pl.BlockSpec((1, 6, D_MODEL), lambda i: (i // tiles_per_seq, 0, 0)),
            pl.BlockSpec((CONV_K, CONV_WIDTH), const),
            pl.BlockSpec((1, CONV_WIDTH), const),
            pl.BlockSpec((ATTN_WIDTH, D_MODEL), const),
            pl.BlockSpec((CONV_WIDTH, D_MODEL), const),
            pl.BlockSpec((1, D_MODEL), const),
            pl.BlockSpec((1, D_MODEL), const),
            pl.BlockSpec((D_MODEL, 2 * ROUTER_LANES), const),
            pl.BlockSpec((1, ROUTER_LANES), const),
        ],
        out_specs=(
            pl.BlockSpec((tm, D_MODEL), lambda i: (i, 0)),
            pl.BlockSpec((tm, D_MODEL), lambda i: (i, 0)),
            pl.BlockSpec((tm, ROUTER_LANES), lambda i: (i, 0)),
            pl.BlockSpec((1, ROUTER_LANES), const),
        ),
        scratch_shapes=[pltpu.VMEM((1, ROUTER_LANES), F32)],
        compiler_params=_cparams(("arbitrary",)),
        name="mix_route",
    )(x2d, attn2d, cb, u, u, u, ada3, conv_w, conv_b, woa, woc, g1, b1, wr, br)


def _row_copy(src, src_row, dst, dst_row, sem):
    return pltpu.make_async_copy(src.at[pl.ds(src_row, 1), :], dst.at[pl.ds(dst_row, 1), :], sem)


def _dispatch_kernel(dest_ref, h2_ref, xs_in_ref, xs_ref, sem):
    del xs_in_ref
    tm = TM_ROW

    def body(t, c):
        _row_copy(h2_ref, t, xs_ref, dest_ref[0, 0, t], sem).start()
        _row_copy(h2_ref, t, xs_ref, dest_ref[0, 0, tm + t], sem).start()
        return c

    lax.fori_loop(0, tm, body, 0, unroll=8)
    for _ in range(2):
        pltpu.make_async_copy(h2_ref, xs_ref.at[pl.ds(0, tm), :], sem).wait()


def _dispatch(dest_tiles, h2, xs_zero):
    t = h2.shape[0]
    tm = TM_ROW
    return pl.pallas_call(
        _dispatch_kernel,
        out_shape=jax.ShapeDtypeStruct(xs_zero.shape, F32),
        grid=(t // tm,),
        in_specs=[
            pl.BlockSpec((1, 1, 2 * tm), lambda i: (i, 0, 0), memory_space=pltpu.SMEM),
            pl.BlockSpec((tm, D_MODEL), lambda i: (i, 0)),
            pl.BlockSpec(memory_space=pl.ANY),
        ],
        out_specs=pl.BlockSpec(memory_space=pl.ANY),
        scratch_shapes=[pltpu.SemaphoreType.DMA],
        input_output_aliases={2: 0},
        compiler_params=_cparams(("arbitrary",)),
        name="dispatch",
    )(dest_tiles, h2, xs_zero)


def _experts_kernel(blk_exp_ref, n_used_ref, xs_ref, wg_ref, wu_ref, wd_ref, y_ref):
    del blk_exp_ref
    used = pl.program_id(0) < n_used_ref[0]

    @pl.when(jnp.logical_not(used))
    def _():
        y_ref[...] = jnp.zeros_like(y_ref)

    @pl.when(used)
    def _():
        x = xs_ref[...].astype(BF16)
        g = _dot(x, wg_ref[0])
        up = _dot(x, wu_ref[0])
        act = (g * jax.nn.sigmoid(g) * up).astype(BF16)
        y_ref[...] = _dot(act, wd_ref[0])


def _experts(blk_exp, n_used, xs, wg, wu, wd):
    rows = xs.shape[0]
    blk = EXPERT_BLOCK
    return pl.pallas_call(
        _experts_kernel,
        out_shape=jax.ShapeDtypeStruct((rows, D_MODEL), F32),
        grid_spec=pltpu.PrefetchScalarGridSpec(
            num_scalar_prefetch=2,
            grid=(rows // blk,),
            in_specs=[
                pl.BlockSpec((blk, D_MODEL), lambda i, be, nu: (i, 0)),
                pl.BlockSpec((1, D_MODEL, D_FF_EXPERT), lambda i, be, nu: (be[i], 0, 0)),
                pl.BlockSpec((1, D_MODEL, D_FF_EXPERT), lambda i, be, nu: (be[i], 0, 0)),
                pl.BlockSpec((1, D_FF_EXPERT, D_MODEL), lambda i, be, nu: (be[i], 0, 0)),
            ],
            out_specs=pl.BlockSpec((blk, D_MODEL), lambda i, be, nu: (i, 0)),
        ),
        compiler_params=_cparams(("arbitrary",)),
        name="experts",
    )(blk_exp, n_used, xs, wg, wu, wd)


def _combine_kernel(dest_ref, x1_ref, rt_ref, ada_ref, g2_ref, b2_ref, y_ref, o_ref, ybuf, sem):
    tm = TM_ROW

    def body(t, c):
        _row_copy(y_ref, dest_ref[0, 0, t], ybuf.at[0], t, sem).start()
        _row_copy(y_ref, dest_ref[0, 0, tm + t], ybuf.at[1], t, sem).start()
        return c

    lax.fori_loop(0, tm, body, 0, unroll=8)
    for k in range(2):
        pltpu.make_async_copy(y_ref.at[pl.ds(0, tm), :], ybuf.at[k], sem).wait()

    g_f = ada_ref[0][5:6, :]
    rt = rt_ref[...]
    w0 = rt[:, 4:5]
    w1 = rt[:, 5:6]
    f = ybuf[0] * w0 + ybuf[1] * w1
    z = DEEPNORM_ALPHA * x1_ref[...] + g_f * f
    o_ref[...] = _layernorm(z) * g2_ref[...] + b2_ref[...]


def _combine(dest_tiles, x1, rt, ada3, g2, b2, y, seq):
    t = x1.shape[0]
    tm = TM_ROW
    tiles_per_seq = seq // tm
    const = lambda i: (0, 0)
    return pl.pallas_call(
        _combine_kernel,
        out_shape=jax.ShapeDtypeStruct((t, D_MODEL), F32),
        grid=(t // tm,),
        in_specs=[
            pl.BlockSpec((1, 1, 2 * tm), lambda i: (i, 0, 0), memory_space=pltpu.SMEM),
            pl.BlockSpec((tm, D_MODEL), lambda i: (i, 0)),
            pl.BlockSpec((tm, ROUTER_LANES), lambda i: (i, 0)),
            pl.BlockSpec((1, 6, D_MODEL), lambda i: (i // tiles_per_seq, 0, 0)),
            pl.BlockSpec((1, D_MODEL), const),
            pl.BlockSpec((1, D_MODEL), const),
            pl.BlockSpec(memory_space=pl.ANY),
        ],
        out_specs=pl.BlockSpec((tm, D_MODEL), lambda i: (i, 0)),
        scratch_shapes=[pltpu.VMEM((2, tm, D_MODEL), F32), pltpu.SemaphoreType.DMA],
        compiler_params=_cparams(("arbitrary",)),
        name="combine",
    )(dest_tiles, x1, rt, ada3, g2, b2, y)


def _rope_tables(seq):
    half = QK_HEAD_DIM // 2
    inv = 1.0 / (ROPE_THETA ** (jnp.arange(0, QK_HEAD_DIM, 2, dtype=F32) / QK_HEAD_DIM))
    ang = jnp.arange(seq, dtype=F32)[:, None] * inv[None, :]
    cos, sin = jnp.cos(ang), jnp.sin(ang)
    cos_t = jnp.tile(cos, (1, LANES // half))
    sin_t = jnp.tile(jnp.concatenate([-sin, sin], axis=1), (1, LANES // QK_HEAD_DIM))
    return cos_t, sin_t


def _routing_tables(rt, counts, n_rows_max):
    t = rt.shape[0]
    blk = EXPERT_BLOCK
    counts = counts[0, EXPERT_LANE0:EXPERT_LANE0 + N_EXPERTS].astype(jnp.int32)
    padded = ((counts + blk - 1) // blk) * blk
    pend = jnp.cumsum(padded)
    pstart = pend - padded
    e = rt[:, 0:2].astype(jnp.int32)
    rank = rt[:, 2:4].astype(jnp.int32)
    dest = pstart[e] + rank
    dest_tiles = dest.reshape(t // TM_ROW, TM_ROW, 2).transpose(0, 2, 1).reshape(t // TM_ROW, 1, 2 * TM_ROW)
    blk_start = jnp.arange(n_rows_max // blk, dtype=jnp.int32) * blk
    blk_exp = jnp.minimum(jnp.searchsorted(pend, blk_start, side='right'), N_EXPERTS - 1).astype(jnp.int32)
    n_used = (pend[-1:] // blk).astype(jnp.int32)
    return dest_tiles, blk_exp, n_used


def _encoder(x, c, p):
    b, seq, d = x.shape
    t = b * seq
    x2d = x.reshape(t, d)
    ada3 = _ada(c, p["w_ada_hi"], p["w_ada_lo"], p["b_ada"]).reshape(b, 6, d)
    cos_t, sin_t = _rope_tables(seq)
    q, k, v, cb, u = _in_proj(x2d, ada3, p["w_in"], cos_t, sin_t, seq)
    attn = _diff_attn(q.reshape(b, seq, 512), k.reshape(b, seq, 512), v.reshape(b, seq, 512),
                      p["lq1"], p["lk1"], p["lq2"], p["lk2"], p["subln_g"])
    x1, h2, rt, counts = _mix_route(x2d, attn.reshape(t, 512), cb, u, ada3, p["conv_w"], p["conv_b"],
                                    p["woa"], p["woc"], p["ln1_g"], p["ln1_b"], p["wr"], p["br"], seq)
    n_rows_max = (2 * t // EXPERT_BLOCK + N_EXPERTS) * EXPERT_BLOCK
    dest_tiles, blk_exp, n_used = _routing_tables(rt, counts, n_rows_max)
    xs = _dispatch(dest_tiles, h2, jnp.zeros((n_rows_max, d), F32))
    y = _experts(blk_exp, n_used, xs, p["wg"], p["wu"], p["wd"])
    out = _combine(dest_tiles, x1, rt, ada3, p["ln2_g"], p["ln2_b"], y, seq)
    return out.reshape(b, seq, d)


def kernel(x_prompt, x_sample, c_prompt, c_sample, w_ada, b_ada, w_in, lambda_q1, lambda_k1, lambda_q2, lambda_k2, attn_subln_g, conv_w, conv_b, w_out, ln1_g, ln1_b, router_group_w, router_group_b, router_expert_w, router_expert_b, expert_w_gate, expert_w_up, expert_w_down, ln2_g, ln2_b):
    l = 0
    w_ada_hi, w_ada_lo = _split_hi_lo(w_ada[l])
    wr = jnp.concatenate([router_group_w[l], router_expert_w[l]], axis=1)
    wr = jnp.pad(wr, ((0, 0), (0, ROUTER_LANES - wr.shape[1])))
    wr_hi, wr_lo = _split_hi_lo(wr)
    br = jnp.concatenate([router_group_b[l], router_expert_b[l]])
    br = jnp.pad(br, (0, ROUTER_LANES - br.shape[0])).reshape(1, ROUTER_LANES)
    w_out_bf = w_out[l].astype(BF16)
    p = {
        "w_ada_hi": w_ada_hi, "w_ada_lo": w_ada_lo, "b_ada": b_ada[l].reshape(1, -1),
        "w_in": w_in[l].astype(BF16),
        "lq1": lambda_q1[l].reshape(1, -1), "lk1": lambda_k1[l].reshape(1, -1),
        "lq2": lambda_q2[l].reshape(1, -1), "lk2": lambda_k2[l].reshape(1, -1),
        "subln_g": attn_subln_g[l].reshape(1, -1),
        "conv_w": conv_w[l], "conv_b": conv_b[l].reshape(1, -1),
        "woa": w_out_bf[:ATTN_WIDTH], "woc": w_out_bf[ATTN_WIDTH:],
        "ln1_g": ln1_g[l].reshape(1, -1), "ln1_b": ln1_b[l].reshape(1, -1),
        "wr": jnp.concatenate([wr_hi, wr_lo], axis=1), "br": br,
        "wg": expert_w_gate[l].astype(BF16), "wu": expert_w_up[l].astype(BF16),
        "wd": expert_w_down[l].astype(BF16),
        "ln2_g": ln2_g[l].reshape(1, -1), "ln2_b": ln2_b[l].reshape(1, -1),
    }
    y_prompt = _encoder(x_prompt, c_prompt, p)
    y_sample = _encoder(x_sample, c_sample, p)
    return (y_prompt, y_sample)
```

```python
import functools
import math

import jax
import jax.numpy as jnp
from jax import lax
from jax.experimental import pallas as pl
from jax.experimental.pallas import tpu as pltpu

D_MODEL = 1024
ATTN_WIDTH = 512
CONV_WIDTH = 512
QK_HEAD_DIM = 64
V_HEAD_DIM = 128
N_DIFF_HEADS = 4
IN_WIDTH = 3072
CONV_K = 3
ROPE_THETA = 10000.0
N_GROUPS = 4
EXPERTS_PER_GROUP = 8
N_EXPERTS = 32
D_FF_EXPERT = 512
LN_EPS = 1e-5
RMS_EPS = 1e-5
DEPTH = 1
DEEPNORM_ALPHA = (2.0 * DEPTH) ** 0.25
LAMBDA_INIT = 0.8 - 0.6 * math.exp(-0.3 * 0)

LANES = 128
BF16_SUBLANES = 16
VMEM_LIMIT = 48 * 1024 * 1024

TM_PROJ = 512
TQ = 512
TK = 512
TM_MIX = 256
TM_ROW = 512
EXPERT_BLOCK = 256
ROUTER_LANES = 128
EXPERT_LANE0 = N_GROUPS

BF16 = jnp.bfloat16
F32 = jnp.float32


def _cparams(sem):
    return pltpu.CompilerParams(dimension_semantics=sem, vmem_limit_bytes=VMEM_LIMIT)


def _layernorm(x):
    mu = jnp.mean(x, axis=-1, keepdims=True)
    xc = x - mu
    var = jnp.mean(xc * xc, axis=-1, keepdims=True)
    return xc * lax.rsqrt(var + LN_EPS)


def _split_hi_lo(a):
    hi = a.astype(BF16)
    lo = (a - hi.astype(F32)).astype(BF16)
    return hi, lo


def _dot(a, b):
    return jnp.dot(a, b, preferred_element_type=F32)


def _ada_kernel(c_ref, whi_ref, wlo_ref, b_ref, o_ref):
    c = c_ref[...]
    s = c * jax.nn.sigmoid(c)
    s_hi, s_lo = _split_hi_lo(s)
    acc = _dot(s_hi, whi_ref[...]) + _dot(s_lo, whi_ref[...]) + _dot(s_hi, wlo_ref[...])
    o_ref[...] = acc + b_ref[...]


def _ada(c, w_hi, w_lo, b):
    bsz = c.shape[0]
    n = w_hi.shape[1]
    tn = 1024
    return pl.pallas_call(
        _ada_kernel,
        out_shape=jax.ShapeDtypeStruct((bsz, n), F32),
        grid=(n // tn,),
        in_specs=[
            pl.BlockSpec((bsz, D_MODEL), lambda j: (0, 0)),
            pl.BlockSpec((D_MODEL, tn), lambda j: (0, j)),
            pl.BlockSpec((D_MODEL, tn), lambda j: (0, j)),
            pl.BlockSpec((1, tn), lambda j: (0, j)),
        ],
        out_specs=pl.BlockSpec((bsz, tn), lambda j: (0, j)),
        compiler_params=_cparams(("arbitrary",)),
        name="ada",
    )(c, w_hi, w_lo, b)


def _rope(x, cos_t, sin_t):
    lane = lax.broadcasted_iota(jnp.int32, x.shape, 1)
    upper = (lane & 32) != 0
    partner = jnp.where(upper, pltpu.roll(x, 32, axis=1), pltpu.roll(x, LANES - 32, axis=1))
    return x * cos_t + partner * sin_t


def _in_proj_kernel(x_ref, ada_ref, w_ref, cos_ref, sin_ref,
                    q_ref, k_ref, v_ref, cb_ref, u_ref):
    x = x_ref[...]
    ada = ada_ref[0]
    sh_m = ada[0:1, :]
    sc_m = ada[1:2, :]
    h = (_layernorm(x) * (1.0 + sc_m) + sh_m).astype(BF16)
    cos_t = cos_ref[...]
    sin_t = sin_ref[...]
    qk_scale = QK_HEAD_DIM ** -0.5 * math.log2(math.e)
    for j in range(4):
        lo = j * LANES
        qj = _dot(h, w_ref[:, lo:lo + LANES])
        q_ref[:, lo:lo + LANES] = (_rope(qj, cos_t, sin_t) * qk_scale).astype(BF16)
        kj = _dot(h, w_ref[:, 512 + lo:512 + lo + LANES])
        k_ref[:, lo:lo + LANES] = _rope(kj, cos_t, sin_t).astype(BF16)
    v_ref[...] = _dot(h, w_ref[:, 1024:1536]).astype(BF16)
    cb_ref[...] = _dot(h, w_ref[:, 1536:2048]).astype(BF16)
    cc = _dot(h, w_ref[:, 2048:2560])
    ch = _dot(h, w_ref[:, 2560:3072])
    u_ref[...] = (cc * ch).astype(BF16)


def _in_proj(x2d, ada3, w_in_bf, cos_t, sin_t, seq):
    t = x2d.shape[0]
    tm = TM_PROJ
    tiles_per_seq = seq // tm
    out = jax.ShapeDtypeStruct((t, 512), BF16)
    ospec = pl.BlockSpec((tm, 512), lambda i: (i, 0))
    return pl.pallas_call(
        _in_proj_kernel,
        out_shape=(out,) * 5,
        grid=(t // tm,),
        in_specs=[
            pl.BlockSpec((tm, D_MODEL), lambda i: (i, 0)),
            pl.BlockSpec((1, 6, D_MODEL), lambda i: (i // tiles_per_seq, 0, 0)),
            pl.BlockSpec((D_MODEL, IN_WIDTH), lambda i: (0, 0)),
            pl.BlockSpec((tm, LANES), lambda i: (i % tiles_per_seq, 0)),
            pl.BlockSpec((tm, LANES), lambda i: (i % tiles_per_seq, 0)),
        ],
        out_specs=(ospec,) * 5,
        compiler_params=_cparams(("arbitrary",)),
        name="in_proj",
    )(x2d, ada3, w_in_bf, cos_t, sin_t)


def _diff_attn_kernel(q_ref, k_ref, v_ref, lq1_ref, lk1_ref, lq2_ref, lk2_ref, g_ref, o_ref,
                      qq_ref, vx_ref, s_buf, m_ref, acc_ref, *, seq):
    q = q_ref[0]
    lane = lax.broadcasted_iota(jnp.int32, q.shape, 1)
    zero = jnp.zeros_like(q)
    qq_ref[0:TQ, :] = jnp.where(lane < QK_HEAD_DIM, q, zero)
    qq_ref[TQ:2 * TQ, :] = jnp.where(lane >= QK_HEAD_DIM, q, zero)
    m_ref[...] = jnp.full(m_ref.shape, -jnp.inf, F32)
    acc_ref[...] = jnp.zeros(acc_ref.shape, F32)
    nk = seq // TK
    n_lane_blocks = TK // LANES

    @pl.when(pl.program_id(2) == 0)
    def _():
        vx_ref[:, 0:V_HEAD_DIM] = v_ref[0]
        vx_ref[:, V_HEAD_DIM:2 * V_HEAD_DIM] = jnp.ones((seq, V_HEAD_DIM), BF16)

    def scores(j, slot):
        start = pl.multiple_of(j * TK, TK)
        kc = k_ref[0, pl.ds(start, TK), :]
        s_buf[slot] = lax.dot_general(qq_ref[...], kc, (((1,), (1,)), ((), ())), preferred_element_type=F32)

    def softmax_pv(j, slot):
        blocks = [s_buf[slot, :, c * LANES:(c + 1) * LANES] for c in range(n_lane_blocks)]
        mb = blocks[0]
        for c in range(1, n_lane_blocks):
            mb = jnp.maximum(mb, blocks[c])
        m_old = m_ref[...]
        m_new = jnp.maximum(m_old, jnp.max(mb, axis=-1, keepdims=True))
        alpha = jnp.exp2(m_old - m_new)
        m_ref[...] = m_new
        p = jnp.concatenate([jnp.exp2(blk - m_new).astype(BF16) for blk in blocks], axis=1)
        start = pl.multiple_of(j * TK, TK)
        pv = _dot(p, vx_ref[pl.ds(start, TK), :])
        acc_ref[:, 0:V_HEAD_DIM] = alpha * acc_ref[:, 0:V_HEAD_DIM] + pv[:, 0:V_HEAD_DIM]
        acc_ref[:, V_HEAD_DIM:] = alpha * acc_ref[:, V_HEAD_DIM:] + pv[:, V_HEAD_DIM:]

    scores(0, 0)

    def pair(jj, c):
        j = 2 * jj
        scores(j + 1, 1)
        softmax_pv(j, 0)
        scores(j + 2, 0)
        softmax_pv(j + 1, 1)
        return c

    lax.fori_loop(0, nk // 2 - 1, pair, 0)
    scores(nk - 1, 1)
    softmax_pv(nk - 2, 0)
    softmax_pv(nk - 1, 1)

    o = acc_ref[:, 0:V_HEAD_DIM] / acc_ref[:, V_HEAD_DIM:]
    lam = (jnp.exp(jnp.sum(lq1_ref[...] * lk1_ref[...], axis=-1, keepdims=True))
           - jnp.exp(jnp.sum(lq2_ref[...] * lk2_ref[...], axis=-1, keepdims=True)) + LAMBDA_INIT)
    of = o[:TQ] - lam * o[TQ:]
    of = of * lax.rsqrt(jnp.mean(of * of, axis=-1, keepdims=True) + RMS_EPS)
    of = of * g_ref[...] * (1.0 - LAMBDA_INIT)
    o_ref[0] = of.astype(BF16)


def _diff_attn(q, k, v, lq1, lk1, lq2, lk2, g):
    b, seq, _ = q.shape
    lam_spec = pl.BlockSpec((1, QK_HEAD_DIM), lambda bi, h, qi: (0, 0))
    return pl.pallas_call(
        functools.partial(_diff_attn_kernel, seq=seq),
        out_shape=jax.ShapeDtypeStruct((b, seq, ATTN_WIDTH), BF16),
        grid=(b, N_DIFF_HEADS, seq // TQ),
        in_specs=[
            pl.BlockSpec((1, TQ, LANES), lambda bi, h, qi: (bi, qi, h)),
            pl.BlockSpec((1, seq, LANES), lambda bi, h, qi: (bi, 0, h)),
            pl.BlockSpec((1, seq, LANES), lambda bi, h, qi: (bi, 0, h)),
            lam_spec, lam_spec, lam_spec, lam_spec,
            pl.BlockSpec((1, V_HEAD_DIM), lambda bi, h, qi: (0, 0)),
        ],
        out_specs=pl.BlockSpec((1, TQ, LANES), lambda bi, h, qi: (bi, qi, h)),
        scratch_shapes=[
            pltpu.VMEM((2 * TQ, LANES), BF16),
            pltpu.VMEM((seq, 2 * V_HEAD_DIM), BF16),
            pltpu.VMEM((2, 2 * TQ, TK), F32),
            pltpu.VMEM((2 * TQ, LANES), F32),
            pltpu.VMEM((2 * TQ, 2 * V_HEAD_DIM), F32),
        ],
        compiler_params=_cparams(("arbitrary", "arbitrary", "arbitrary")),
        name="diff_attn",
    )(q, k, v, lq1, lk1, lq2, lk2, g)


def _lane_min_index(mask, lane_f):
    return jnp.min(jnp.where(mask, lane_f, float(ROUTER_LANES)), axis=-1, keepdims=True)


def _mix_route_kernel(x_ref, attn_ref, cb_ref, u_ref, uprev_ref, unext_ref, ada_ref,
                      cw_ref, cbias_ref, woa_ref, woc_ref, g1_ref, b1_ref, wr_ref, br_ref,
                      x1_ref, h2_ref, rt_ref, cnt_ref, base_ref, *, seq):
    i = pl.program_id(0)
    tm = TM_MIX

    @pl.when(i == 0)
    def _():
        base_ref[...] = jnp.zeros_like(base_ref)

    ada = ada_ref[0]
    g_m = ada[2:3, :]
    sh_f = ada[3:4, :]
    sc_f = ada[4:5, :]

    u = u_ref[...].astype(F32)
    row = lax.broadcasted_iota(jnp.int32, u.shape, 0)
    not_seq_start = ((i * tm) % seq != 0).astype(F32)
    not_seq_end = (((i + 1) * tm) % seq != 0).astype(F32)
    halo_prev = uprev_ref[...].astype(F32)[BF16_SUBLANES - 1:BF16_SUBLANES, :] * not_seq_start
    halo_next = unext_ref[...].astype(F32)[0:1, :] * not_seq_end
    u_prev = jnp.where(row == 0, halo_prev, pltpu.roll(u, 1, axis=0))
    u_next = jnp.where(row == tm - 1, halo_next, pltpu.roll(u, tm - 1, axis=0))
    cw = cw_ref[...]
    y = cbias_ref[...] + u_prev * cw[0:1, :]
    y = y + u * cw[1:2, :]
    y = y + u_next * cw[2:3, :]
    conv = (cb_ref[...].astype(F32) * y).astype(BF16)

    mix = _dot(attn_ref[...], woa_ref[...]) + _dot(conv, woc_ref[...])
    x1 = _layernorm(DEEPNORM_ALPHA * x_ref[...] + g_m * mix) * g1_ref[...] + b1_ref[...]
    x1_ref[...] = x1
    h2 = _layernorm(x1) * (1.0 + sc_f) + sh_f
    h2_ref[...] = h2

    h_hi, h_lo = _split_hi_lo(h2)
    both = _dot(h_hi, wr_ref[...])
    logits = (both[:, :ROUTER_LANES] + both[:, ROUTER_LANES:]
              + _dot(h_lo, wr_ref[:, :ROUTER_LANES]) + br_ref[...])

    lane = lax.broadcasted_iota(jnp.int32, logits.shape, 1)
    lane_f = lane.astype(F32)
    neg = -jnp.inf
    is_group = lane < N_GROUPS
    lg = jnp.where(is_group, logits, neg)
    lg_max = jnp.max(lg, axis=-1, keepdims=True)
    g_sel = _lane_min_index(lg == lg_max, lane_f)
    pg_sel = 1.0 / jnp.sum(jnp.exp(lg - lg_max), axis=-1, keepdims=True)

    first = EXPERT_LANE0 + EXPERTS_PER_GROUP * g_sel
    in_group = (lane_f >= first) & (lane_f < first + EXPERTS_PER_GROUP)
    le = jnp.where(in_group, logits, neg)
    l0 = jnp.max(le, axis=-1, keepdims=True)
    i0 = _lane_min_index(le == l0, lane_f)
    le2 = jnp.where(lane_f == i0, neg, le)
    l1 = jnp.max(le2, axis=-1, keepdims=True)
    i1 = _lane_min_index(le2 == l1, lane_f)
    t_exp = jnp.exp(l1 - l0)
    p0 = 1.0 / (1.0 + t_exp)
    w0 = pg_sel * p0
    w1 = pg_sel * (t_exp * p0)

    oh0 = lane_f == i0
    oh1 = lane_f == i1
    onehots = jnp.concatenate([jnp.where(oh0, 1.0, 0.0), jnp.where(oh1, 1.0, 0.0)], axis=1).astype(BF16)
    r_i = lax.broadcasted_iota(jnp.int32, (tm, tm), 0)
    c_i = lax.broadcasted_iota(jnp.int32, (tm, tm), 1)
    tri = jnp.where(c_i < r_i, 1.0, 0.0).astype(BF16)
    before = _dot(tri, onehots)
    cnt0 = jnp.sum(jnp.where(oh0, 1.0, 0.0), axis=0, keepdims=True)
    cnt1 = jnp.sum(jnp.where(oh1, 1.0, 0.0), axis=0, keepdims=True)
    base = base_ref[...]
    rank0 = jnp.sum(jnp.where(oh0, before[:, :ROUTER_LANES] + base, 0.0), axis=-1, keepdims=True)
    rank1 = jnp.sum(jnp.where(oh1, before[:, ROUTER_LANES:] + base + cnt0, 0.0), axis=-1, keepdims=True)
    new_base = base + cnt0 + cnt1
    base_ref[...] = new_base
    cnt_ref[...] = new_base

    rt = jnp.where(lane == 0, i0 - EXPERT_LANE0, 0.0)
    rt = jnp.where(lane == 1, i1 - EXPERT_LANE0, rt)
    rt = jnp.where(lane == 2, rank0, rt)
    rt = jnp.where(lane == 3, rank1, rt)
    rt = jnp.where(lane == 4, w0, rt)
    rt = jnp.where(lane == 5, w1, rt)
    rt_ref[...] = rt


def _mix_route(x2d, attn2d, cb, u, ada3, conv_w, conv_b, woa, woc, g1, b1, wr, br, seq):
    t = x2d.shape[0]
    tm = TM_MIX
    tiles_per_seq = seq // tm
    hb = tm // BF16_SUBLANES
    n_halo = t // BF16_SUBLANES
    const = lambda i: (0, 0)
    return pl.pallas_call(
        functools.partial(_mix_route_kernel, seq=seq),
        out_shape=(
            jax.ShapeDtypeStruct((t, D_MODEL), F32),
            jax.ShapeDtypeStruct((t, D_MODEL), F32),
            jax.ShapeDtypeStruct((t, ROUTER_LANES), F32),
            jax.ShapeDtypeStruct((1, ROUTER_LANES), F32),
        ),
        grid=(t // tm,),
        in_specs=[
            pl.BlockSpec((tm, D_MODEL), lambda i: (i, 0)),
            pl.BlockSpec((tm, ATTN_WIDTH), lambda i: (i, 0)),
            pl.BlockSpec((tm, CONV_WIDTH), lambda i: (i, 0)),
            pl.BlockSpec((tm, CONV_WIDTH), lambda i: (i, 0)),
            pl.BlockSpec((BF16_SUBLANES, CONV_WIDTH), lambda i: (jnp.maximum(i * hb - 1, 0), 0)),
            pl.BlockSpec((BF16_SUBLANES, CONV_WIDTH), lambda i: (jnp.minimum((i + 1) * hb, n_halo - 1), 0)),
            pl.BlockSpec((1, 6, D_MODEL), lambda i: (i // tiles_per_seq, 0, 0)),
            pl.BlockSpec((CONV_K, CONV_WIDTH), const),
            pl.BlockSpec((1, CONV_WIDTH), const),
            pl.BlockSpec((ATTN_WIDTH, D_MODEL), const),
            pl.BlockSpec((CONV_WIDTH, D_MODEL), const),
            pl.BlockSpec((1, D_MODEL), const),
            pl.BlockSpec((1, D_MODEL), const),
            pl.BlockSpec((D_MODEL, 2 * ROUTER_LANES), const),
            pl.BlockSpec((1, ROUTER_LANES), const),
        ],
        out_specs=(
            pl.BlockSpec((tm, D_MODEL), lambda i: (i, 0)),
            pl.BlockSpec((tm, D_MODEL), lambda i: (i, 0)),
            pl.BlockSpec((tm, ROUTER_LANES), lambda i: (i, 0)),
            pl.BlockSpec((1, ROUTER_LANES), const),
        ),
        scratch_shapes=[pltpu.VMEM((1, ROUTER_LANES), F32)],
        compiler_params=_cparams(("arbitrary",)),
        name="mix_route",
    )(x2d, attn2d, cb, u, u, u, ada3, conv_w, conv_b, woa, woc, g1, b1, wr, br)


def _row_copy(src, src_row, dst, dst_row, sem):
    return pltpu.make_async_copy(src.at[pl.ds(src_row, 1), :], dst.at[pl.ds(dst_row, 1), :], sem)


def _dispatch_kernel(dest_ref, h2_ref, xs_in_ref, xs_ref, sem):
    del xs_in_ref
    tm = TM_ROW

    def body(t, c):
        _row_copy(h2_ref, t, xs_ref, dest_ref[0, 0, t], sem).start()
        _row_copy(h2_ref, t, xs_ref, dest_ref[0, 0, tm + t], sem).start()
        return c

    lax.fori_loop(0, tm, body, 0, unroll=8)
    for _ in range(2):
        pltpu.make_async_copy(h2_ref, xs_ref.at[pl.ds(0, tm), :], sem).wait()


def _dispatch(dest_tiles, h2, xs_zero):
    t = h2.shape[0]
    tm = TM_ROW
    return pl.pallas_call(
        _dispatch_kernel,
        out_shape=jax.ShapeDtypeStruct(xs_zero.shape, F32),
        grid=(t // tm,),
        in_specs=[
            pl.BlockSpec((1, 1, 2 * tm), lambda i: (i, 0, 0), memory_space=pltpu.SMEM),
            pl.BlockSpec((tm, D_MODEL), lambda i: (i, 0)),
            pl.BlockSpec(memory_space=pl.ANY),
        ],
        out_specs=pl.BlockSpec(memory_space=pl.ANY),
        scratch_shapes=[pltpu.SemaphoreType.DMA],
        input_output_aliases={2: 0},
        compiler_params=_cparams(("arbitrary",)),
        name="dispatch",
    )(dest_tiles, h2, xs_zero)


def _experts_kernel(blk_exp_ref, n_used_ref, xs_ref, wg_ref, wu_ref, wd_ref, y_ref):
    del blk_exp_ref
    used = pl.program_id(0) < n_used_ref[0]

    @pl.when(jnp.logical_not(used))
    def _():
        y_ref[...] = jnp.zeros_like(y_ref)

    @pl.when(used)
    def _():
        x = xs_ref[...].astype(BF16)
        g = _dot(x, wg_ref[0])
        up = _dot(x, wu_ref[0])
        act = (g * jax.nn.sigmoid(g) * up).astype(BF16)
        y_ref[...] = _dot(act, wd_ref[0])


def _experts(blk_exp, n_used, xs, wg, wu, wd):
    rows = xs.shape[0]
    blk = EXPERT_BLOCK
    return pl.pallas_call(
        _experts_kernel,
        out_shape=jax.ShapeDtypeStruct((rows, D_MODEL), F32),
        grid_spec=pltpu.PrefetchScalarGridSpec(
            num_scalar_prefetch=2,
            grid=(rows // blk,),
            in_specs=[
                pl.BlockSpec((blk, D_MODEL), lambda i, be, nu: (i, 0)),
                pl.BlockSpec((1, D_MODEL, D_FF_EXPERT), lambda i, be, nu: (be[i], 0, 0)),
                pl.BlockSpec((1, D_MODEL, D_FF_EXPERT), lambda i, be, nu: (be[i], 0, 0)),
                pl.BlockSpec((1, D_FF_EXPERT, D_MODEL), lambda i, be, nu: (be[i], 0, 0)),
            ],
            out_specs=pl.BlockSpec((blk, D_MODEL), lambda i, be, nu: (i, 0)),
        ),
        compiler_params=_cparams(("arbitrary",)),
        name="experts",
    )(blk_exp, n_used, xs, wg, wu, wd)


def _combine_kernel(dest_ref, x1_ref, rt_ref, ada_ref, g2_ref, b2_ref, y_ref, o_ref, ybuf, sem):
    tm = TM_ROW

    def body(t, c):
        _row_copy(y_ref, dest_ref[0, 0, t], ybuf.at[0], t, sem).start()
        _row_copy(y_ref, dest_ref[0, 0, tm + t], ybuf.at[1], t, sem).start()
        return c

    lax.fori_loop(0, tm, body, 0, unroll=8)
    for k in range(2):
        pltpu.make_async_copy(y_ref.at[pl.ds(0, tm), :], ybuf.at[k], sem).wait()

    g_f = ada_ref[0][5:6, :]
    rt = rt_ref[...]
    w0 = rt[:, 4:5]
    w1 = rt[:, 5:6]
    f = ybuf[0] * w0 + ybuf[1] * w1
    z = DEEPNORM_ALPHA * x1_ref[...] + g_f * f
    o_ref[...] = _layernorm(z) * g2_ref[...] + b2_ref[...]


def _combine(dest_tiles, x1, rt, ada3, g2, b2, y, seq):
    t = x1.shape[0]
    tm = TM_ROW
    tiles_per_seq = seq // tm
    const = lambda i: (0, 0)
    return pl.pallas_call(
        _combine_kernel,
        out_shape=jax.ShapeDtypeStruct((t, D_MODEL), F32),
        grid=(t // tm,),
        in_specs=[
            pl.BlockSpec((1, 1, 2 * tm), lambda i: (i, 0, 0), memory_space=pltpu.SMEM),
            pl.BlockSpec((tm, D_MODEL), lambda i: (i, 0)),
            pl.BlockSpec((tm, ROUTER_LANES), lambda i: (i, 0)),
            pl.BlockSpec((1, 6, D_MODEL), lambda i: (i // tiles_per_seq, 0, 0)),
            pl.BlockSpec((1, D_MODEL), const),
            pl.BlockSpec((1, D_MODEL), const),
            pl.BlockSpec(memory_space=pl.ANY),
        ],
        out_specs=pl.BlockSpec((tm, D_MODEL), lambda i: (i, 0)),
        scratch_shapes=[pltpu.VMEM((2, tm, D_MODEL), F32), pltpu.SemaphoreType.DMA],
        compiler_params=_cparams(("arbitrary",)),
        name="combine",
    )(dest_tiles, x1, rt, ada3, g2, b2, y)


def _rope_tables(seq):
    half = QK_HEAD_DIM // 2
    inv = 1.0 / (ROPE_THETA ** (jnp.arange(0, QK_HEAD_DIM, 2, dtype=F32) / QK_HEAD_DIM))
    ang = jnp.arange(seq, dtype=F32)[:, None] * inv[None, :]
    cos, sin = jnp.cos(ang), jnp.sin(ang)
    cos_t = jnp.tile(cos, (1, LANES // half))
    sin_t = jnp.tile(jnp.concatenate([-sin, sin], axis=1), (1, LANES // QK_HEAD_DIM))
    return cos_t, sin_t


def _routing_tables(rt, counts, n_rows_max):
    t = rt.shape[0]
    blk = EXPERT_BLOCK
    counts = counts[0, EXPERT_LANE0:EXPERT_LANE0 + N_EXPERTS].astype(jnp.int32)
    padded = ((counts + blk - 1) // blk) * blk
    pend = jnp.cumsum(padded)
    pstart = pend - padded
    e = rt[:, 0:2].astype(jnp.int32)
    rank = rt[:, 2:4].astype(jnp.int32)
    dest = pstart[e] + rank
    dest_tiles = dest.reshape(t // TM_ROW, TM_ROW, 2).transpose(0, 2, 1).reshape(t // TM_ROW, 1, 2 * TM_ROW)
    blk_start = jnp.arange(n_rows_max // blk, dtype=jnp.int32) * blk
    blk_exp = jnp.minimum(jnp.sum(blk_start[:, None] >= pend[None, :], axis=1), N_EXPERTS - 1).astype(jnp.int32)
    n_used = (pend[-1:] // blk).astype(jnp.int32)
    return dest_tiles, blk_exp, n_used


def _encoder(x, c, p):
    b, seq, d = x.shape
    t = b * seq
    x2d = x.reshape(t, d)
    ada3 = _ada(c, p["w_ada_hi"], p["w_ada_lo"], p["b_ada"]).reshape(b, 6, d)
    cos_t, sin_t = _rope_tables(seq)
    q, k, v, cb, u = _in_proj(x2d, ada3, p["w_in"], cos_t, sin_t, seq)
    attn = _diff_attn(q.reshape(b, seq, 512), k.reshape(b, seq, 512), v.reshape(b, seq, 512),
                      p["lq1"], p["lk1"], p["lq2"], p["lk2"], p["subln_g"])
    x1, h2, rt, counts = _mix_route(x2d, attn.reshape(t, 512), cb, u, ada3, p["conv_w"], p["conv_b"],
                                    p["woa"], p["woc"], p["ln1_g"], p["ln1_b"], p["wr"], p["br"], seq)
    n_rows_max = (2 * t // EXPERT_BLOCK + N_EXPERTS) * EXPERT_BLOCK
    dest_tiles, blk_exp, n_used = _routing_tables(rt, counts, n_rows_max)
    xs = _dispatch(dest_tiles, h2, jnp.zeros((n_rows_max, d), F32))
    y = _experts(blk_exp, n_used, xs, p["wg"], p["wu"], p["wd"])
    out = _combine(dest_tiles, x1, rt, ada3, p["ln2_g"], p["ln2_b"], y, seq)
    return out.reshape(b, seq, d)


def kernel(x_prompt, x_sample, c_prompt, c_sample, w_ada, b_ada, w_in, lambda_q1, lambda_k1, lambda_q2, lambda_k2, attn_subln_g, conv_w, conv_b, w_out, ln1_g, ln1_b, router_group_w, router_group_b, router_expert_w, router_expert_b, expert_w_gate, expert_w_up, expert_w_down, ln2_g, ln2_b):
    l = 0
    w_ada_hi, w_ada_lo = _split_hi_lo(w_ada[l])
    wr = jnp.concatenate([router_group_w[l], router_expert_w[l]], axis=1)
    wr = jnp.pad(wr, ((0, 0), (0, ROUTER_LANES - wr.shape[1])))
    wr_hi, wr_lo = _split_hi_lo(wr)
    br = jnp.concatenate([router_group_b[l], router_expert_b[l]])
    br = jnp.pad(br, (0, ROUTER_LANES - br.shape[0])).reshape(1, ROUTER_LANES)
    w_out_bf = w_out[l].astype(BF16)
    p = {
        "w_ada_hi": w_ada_hi, "w_ada_lo": w_ada_lo, "b_ada": b_ada[l].reshape(1, -1),
        "w_in": w_in[l].astype(BF16),
        "lq1": lambda_q1[l].reshape(1, -1), "lk1": lambda_k1[l].reshape(1, -1),
        "lq2": lambda_q2[l].reshape(1, -1), "lk2": lambda_k2[l].reshape(1, -1),
        "subln_g": attn_subln_g[l].reshape(1, -1),
        "conv_w": conv_w[l], "conv_b": conv_b[l].reshape(1, -1),
        "woa": w_out_bf[:ATTN_WIDTH], "woc": w_out_bf[ATTN_WIDTH:],
        "ln1_g": ln1_g[l].reshape(1, -1), "ln1_b": ln1_b[l].reshape(1, -1),
        "wr": jnp.concatenate([wr_hi, wr_lo], axis=1), "br": br,
        "wg": expert_w_gate[l].astype(BF16), "wu": expert_w_up[l].astype(BF16),
        "wd": expert_w_down[l].astype(BF16),
        "ln2_g": ln2_g[l].reshape(1, -1), "ln2_b": ln2_b[l].reshape(1, -1),
    }
    y_prompt = _encoder(x_prompt, c_prompt, p)
    y_sample = _encoder(x_sample, c_sample, p)
    return (y_prompt, y_sample)
```

```python
import functools
import math

import jax
import jax.numpy as jnp
from jax import lax
from jax.experimental import pallas as pl
from jax.experimental.pallas import tpu as pltpu

D_MODEL = 1024
ATTN_WIDTH = 512
CONV_WIDTH = 512
QK_HEAD_DIM = 64
V_HEAD_DIM = 128
N_DIFF_HEADS = 4
IN_WIDTH = 3072
CONV_K = 3
ROPE_THETA = 10000.0
N_GROUPS = 4
EXPERTS_PER_GROUP = 8
N_EXPERTS = 32
D_FF_EXPERT = 512
LN_EPS = 1e-5
RMS_EPS = 1e-5
DEPTH = 1
DEEPNORM_ALPHA = (2.0 * DEPTH) ** 0.25
LAMBDA_INIT = 0.8 - 0.6 * math.exp(-0.3 * 0)

LANES = 128
BF16_SUBLANES = 16
VMEM_LIMIT = 48 * 1024 * 1024

TM_PROJ = 512
TQ = 512
TK = 512
TM_MIX = 512
TM_ROW = 512
EXPERT_BLOCK = 512
ROUTER_LANES = 128
EXPERT_LANE0 = N_GROUPS

BF16 = jnp.bfloat16
F32 = jnp.float32


def _cparams(sem):
    return pltpu.CompilerParams(dimension_semantics=sem, vmem_limit_bytes=VMEM_LIMIT)


def _layernorm(x):
    mu = jnp.mean(x, axis=-1, keepdims=True)
    xc = x - mu
    var = jnp.mean(xc * xc, axis=-1, keepdims=True)
    return xc * lax.rsqrt(var + LN_EPS)


def _split_hi_lo(a):
    hi = a.astype(BF16)
    lo = (a - hi.astype(F32)).astype(BF16)
    return hi, lo


def _dot(a, b):
    return jnp.dot(a, b, preferred_element_type=F32)


HALF = D_MODEL // 2
U32 = jnp.uint32


def _pack_rows(x):
    lo = lax.bitcast_convert_type(x[:, :HALF].astype(BF16).astype(F32), U32)
    hi = lax.bitcast_convert_type(x[:, HALF:].astype(BF16).astype(F32), U32)
    return (lo >> 16) | hi


def _unpack_rows(w):
    lo = lax.bitcast_convert_type(w << 16, F32)
    hi = lax.bitcast_convert_type(w & jnp.uint32(0xFFFF0000), F32)
    return lo, hi


def _ada_kernel(c_ref, whi_ref, wlo_ref, b_ref, o_ref):
    c = c_ref[...]
    s = c * jax.nn.sigmoid(c)
    s_hi, s_lo = _split_hi_lo(s)
    acc = _dot(s_hi, whi_ref[...]) + _dot(s_lo, whi_ref[...]) + _dot(s_hi, wlo_ref[...])
    o_ref[...] = acc + b_ref[...]


def _ada(c, w_hi, w_lo, b):
    bsz = c.shape[0]
    n = w_hi.shape[1]
    tn = 1024
    return pl.pallas_call(
        _ada_kernel,
        out_shape=jax.ShapeDtypeStruct((bsz, n), F32),
        grid=(n // tn,),
        in_specs=[
            pl.BlockSpec((bsz, D_MODEL), lambda j: (0, 0)),
            pl.BlockSpec((D_MODEL, tn), lambda j: (0, j)),
            pl.BlockSpec((D_MODEL, tn), lambda j: (0, j)),
            pl.BlockSpec((1, tn), lambda j: (0, j)),
        ],
        out_specs=pl.BlockSpec((bsz, tn), lambda j: (0, j)),
        compiler_params=_cparams(("arbitrary",)),
        name="ada",
    )(c, w_hi, w_lo, b)


def _rope(x, cos_t, sin_t):
    lane = lax.broadcasted_iota(jnp.int32, x.shape, 1)
    upper = (lane & 32) != 0
    partner = jnp.where(upper, pltpu.roll(x, 32, axis=1), pltpu.roll(x, LANES - 32, axis=1))
    return x * cos_t + partner * sin_t


def _in_proj_kernel(x_ref, ada_ref, w_ref, cos_ref, sin_ref,
                    q_ref, k_ref, v_ref, cb_ref, u_ref):
    x = x_ref[...]
    ada = ada_ref[0]
    sh_m = ada[0:1, :]
    sc_m = ada[1:2, :]
    h = (_layernorm(x) * (1.0 + sc_m) + sh_m).astype(BF16)
    cos_t = cos_ref[...]
    sin_t = sin_ref[...]
    qk_scale = QK_HEAD_DIM ** -0.5 * math.log2(math.e)
    for j in range(4):
        lo = j * LANES
        qj = _dot(h, w_ref[:, lo:lo + LANES])
        q_ref[:, lo:lo + LANES] = (_rope(qj, cos_t, sin_t) * qk_scale).astype(BF16)
        kj = _dot(h, w_ref[:, 512 + lo:512 + lo + LANES])
        k_ref[:, lo:lo + LANES] = _rope(kj, cos_t, sin_t).astype(BF16)
    v_ref[...] = _dot(h, w_ref[:, 1024:1536]).astype(BF16)
    cb_ref[...] = _dot(h, w_ref[:, 1536:2048]).astype(BF16)
    cc = _dot(h, w_ref[:, 2048:2560])
    ch = _dot(h, w_ref[:, 2560:3072])
    u_ref[...] = (cc * ch).astype(BF16)


def _in_proj(x2d, ada3, w_in_bf, cos_t, sin_t, seq):
    t = x2d.shape[0]
    tm = TM_PROJ
    tiles_per_seq = seq // tm
    out = jax.ShapeDtypeStruct((t, 512), BF16)
    ospec = pl.BlockSpec((tm, 512), lambda i: (i, 0))
    return pl.pallas_call(
        _in_proj_kernel,
        out_shape=(out,) * 5,
        grid=(t // tm,),
        in_specs=[
            pl.BlockSpec((tm, D_MODEL), lambda i: (i, 0)),
            pl.BlockSpec((1, 6, D_MODEL), lambda i: (i // tiles_per_seq, 0, 0)),
            pl.BlockSpec((D_MODEL, IN_WIDTH), lambda i: (0, 0)),
            pl.BlockSpec((tm, LANES), lambda i: (i % tiles_per_seq, 0)),
            pl.BlockSpec((tm, LANES), lambda i: (i % tiles_per_seq, 0)),
        ],
        out_specs=(ospec,) * 5,
        compiler_params=_cparams(("arbitrary",)),
        name="in_proj",
    )(x2d, ada3, w_in_bf, cos_t, sin_t)


def _diff_attn_kernel(q_ref, k_ref, v_ref, lq1_ref, lk1_ref, lq2_ref, lk2_ref, g_ref, o_ref,
                      qq_ref, vx_ref, s_buf, m_ref, acc_ref, *, seq):
    q = q_ref[0]
    lane = lax.broadcasted_iota(jnp.int32, q.shape, 1)
    zero = jnp.zeros_like(q)
    qq_ref[0:TQ, :] = jnp.where(lane < QK_HEAD_DIM, q, zero)
    qq_ref[TQ:2 * TQ, :] = jnp.where(lane >= QK_HEAD_DIM, q, zero)
    m_ref[...] = jnp.full(m_ref.shape, -jnp.inf, F32)
    acc_ref[...] = jnp.zeros(acc_ref.shape, F32)
    nk = seq // TK
    n_lane_blocks = TK // LANES

    @pl.when(pl.program_id(2) == 0)
    def _():
        vx_ref[:, 0:V_HEAD_DIM] = v_ref[0]
        vx_ref[:, V_HEAD_DIM:2 * V_HEAD_DIM] = jnp.ones((seq, V_HEAD_DIM), BF16)

    def scores(j, slot):
        start = pl.multiple_of(j * TK, TK)
        kc = k_ref[0, pl.ds(start, TK), :]
        s_buf[slot] = lax.dot_general(qq_ref[...], kc, (((1,), (1,)), ((), ())), preferred_element_type=F32)

    def softmax_pv(j, slot):
        blocks = [s_buf[slot, :, c * LANES:(c + 1) * LANES] for c in range(n_lane_blocks)]
        mb = blocks[0]
        for c in range(1, n_lane_blocks):
            mb = jnp.maximum(mb, blocks[c])
        m_old = m_ref[...]
        m_new = jnp.maximum(m_old, jnp.max(mb, axis=-1, keepdims=True))
        alpha = jnp.exp2(m_old - m_new)
        m_ref[...] = m_new
        p = jnp.concatenate([jnp.exp2(blk - m_new).astype(BF16) for blk in blocks], axis=1)
        start = pl.multiple_of(j * TK, TK)
        pv = _dot(p, vx_ref[pl.ds(start, TK), :])
        acc_ref[:, 0:V_HEAD_DIM] = alpha * acc_ref[:, 0:V_HEAD_DIM] + pv[:, 0:V_HEAD_DIM]
        acc_ref[:, V_HEAD_DIM:] = alpha * acc_ref[:, V_HEAD_DIM:] + pv[:, V_HEAD_DIM:]

    scores(0, 0)

    def pair(jj, c):
        j = 2 * jj
        scores(j + 1, 1)
        softmax_pv(j, 0)
        scores(j + 2, 0)
        softmax_pv(j + 1, 1)
        return c

    lax.fori_loop(0, nk // 2 - 1, pair, 0)
    scores(nk - 1, 1)
    softmax_pv(nk - 2, 0)
    softmax_pv(nk - 1, 1)

    o = acc_ref[:, 0:V_HEAD_DIM] / acc_ref[:, V_HEAD_DIM:]
    lam = (jnp.exp(jnp.sum(lq1_ref[...] * lk1_ref[...], axis=-1, keepdims=True))
           - jnp.exp(jnp.sum(lq2_ref[...] * lk2_ref[...], axis=-1, keepdims=True)) + LAMBDA_INIT)
    of = o[:TQ] - lam * o[TQ:]
    of = of * lax.rsqrt(jnp.mean(of * of, axis=-1, keepdims=True) + RMS_EPS)
    of = of * g_ref[...] * (1.0 - LAMBDA_INIT)
    o_ref[0] = of.astype(BF16)


def _diff_attn(q, k, v, lq1, lk1, lq2, lk2, g):
    b, seq, _ = q.shape
    lam_spec = pl.BlockSpec((1, QK_HEAD_DIM), lambda bi, h, qi: (0, 0))
    return pl.pallas_call(
        functools.partial(_diff_attn_kernel, seq=seq),
        out_shape=jax.ShapeDtypeStruct((b, seq, ATTN_WIDTH), BF16),
        grid=(b, N_DIFF_HEADS, seq // TQ),
        in_specs=[
            pl.BlockSpec((1, TQ, LANES), lambda bi, h, qi: (bi, qi, h)),
            pl.BlockSpec((1, seq, LANES), lambda bi, h, qi: (bi, 0, h)),
            pl.BlockSpec((1, seq, LANES), lambda bi, h, qi: (bi, 0, h)),
            lam_spec, lam_spec, lam_spec, lam_spec,
            pl.BlockSpec((1, V_HEAD_DIM), lambda bi, h, qi: (0, 0)),
        ],
        out_specs=pl.BlockSpec((1, TQ, LANES), lambda bi, h, qi: (bi, qi, h)),
        scratch_shapes=[
            pltpu.VMEM((2 * TQ, LANES), BF16),
            pltpu.VMEM((seq, 2 * V_HEAD_DIM), BF16),
            pltpu.VMEM((2, 2 * TQ, TK), F32),
            pltpu.VMEM((2 * TQ, LANES), F32),
            pltpu.VMEM((2 * TQ, 2 * V_HEAD_DIM), F32),
        ],
        compiler_params=_cparams(("arbitrary", "arbitrary", "arbitrary")),
        name="diff_attn",
    )(q, k, v, lq1, lk1, lq2, lk2, g)


def _lane_min_index(mask, lane_f):
    return jnp.min(jnp.where(mask, lane_f, float(ROUTER_LANES)), axis=-1, keepdims=True)


def _mix_route_kernel(x_ref, attn_ref, cb_ref, u_ref, uprev_ref, unext_ref, ada_ref,
                      cw_ref, cbias_ref, woa_ref, woc_ref, g1_ref, b1_ref, wr_ref, br_ref,
                      x1_ref, h2_ref, rt_ref, cnt_ref, base_ref, *, seq):
    i = pl.program_id(0)
    tm = TM_MIX

    @pl.when(i == 0)
    def _():
        base_ref[...] = jnp.zeros_like(base_ref)

    ada = ada_ref[0]
    g_m = ada[2:3, :]
    sh_f = ada[3:4, :]
    sc_f = ada[4:5, :]

    u = u_ref[...].astype(F32)
    row = lax.broadcasted_iota(jnp.int32, u.shape, 0)
    not_seq_start = ((i * tm) % seq != 0).astype(F32)
    not_seq_end = (((i + 1) * tm) % seq != 0).astype(F32)
    halo_prev = uprev_ref[...].astype(F32)[BF16_SUBLANES - 1:BF16_SUBLANES, :] * not_seq_start
    halo_next = unext_ref[...].astype(F32)[0:1, :] * not_seq_end
    u_prev = jnp.where(row == 0, halo_prev, pltpu.roll(u, 1, axis=0))
    u_next = jnp.where(row == tm - 1, halo_next, pltpu.roll(u, tm - 1, axis=0))
    cw = cw_ref[...]
    y = cbias_ref[...] + u_prev * cw[0:1, :]
    y = y + u * cw[1:2, :]
    y = y + u_next * cw[2:3, :]
    conv = (cb_ref[...].astype(F32) * y).astype(BF16)

    mix = _dot(attn_ref[...], woa_ref[...]) + _dot(conv, woc_ref[...])
    x1 = _layernorm(DEEPNORM_ALPHA * x_ref[...] + g_m * mix) * g1_ref[...] + b1_ref[...]
    x1_ref[...] = x1
    h2 = _layernorm(x1) * (1.0 + sc_f) + sh_f
    h2_ref[...] = _pack_rows(h2)

    h_hi, h_lo = _split_hi_lo(h2)
    both = _dot(h_hi, wr_ref[...])
    logits = (both[:, :ROUTER_LANES] + both[:, ROUTER_LANES:]
              + _dot(h_lo, wr_ref[:, :ROUTER_LANES]) + br_ref[...])

    lane = lax.broadcasted_iota(jnp.int32, logits.shape, 1)
    lane_f = lane.astype(F32)
    neg = -jnp.inf
    is_group = lane < N_GROUPS
    lg = jnp.where(is_group, logits, neg)
    lg_max = jnp.max(lg, axis=-1, keepdims=True)
    g_sel = _lane_min_index(lg == lg_max, lane_f)
    pg_sel = 1.0 / jnp.sum(jnp.exp(lg - lg_max), axis=-1, keepdims=True)

    first = EXPERT_LANE0 + EXPERTS_PER_GROUP * g_sel
    in_group = (lane_f >= first) & (lane_f < first + EXPERTS_PER_GROUP)
    le = jnp.where(in_group, logits, neg)
    l0 = jnp.max(le, axis=-1, keepdims=True)
    i0 = _lane_min_index(le == l0, lane_f)
    le2 = jnp.where(lane_f == i0, neg, le)
    l1 = jnp.max(le2, axis=-1, keepdims=True)
    i1 = _lane_min_index(le2 == l1, lane_f)
    t_exp = jnp.exp(l1 - l0)
    p0 = 1.0 / (1.0 + t_exp)
    w0 = pg_sel * p0
    w1 = pg_sel * (t_exp * p0)

    oh0 = lane_f == i0
    oh1 = lane_f == i1
    onehots = jnp.concatenate([jnp.where(oh0, 1.0, 0.0), jnp.where(oh1, 1.0, 0.0)], axis=1).astype(BF16)
    r_i = lax.broadcasted_iota(jnp.int32, (tm, tm), 0)
    c_i = lax.broadcasted_iota(jnp.int32, (tm, tm), 1)
    tri = jnp.where(c_i < r_i, 1.0, 0.0).astype(BF16)
    before = _dot(tri, onehots)
    cnt0 = jnp.sum(jnp.where(oh0, 1.0, 0.0), axis=0, keepdims=True)
    cnt1 = jnp.sum(jnp.where(oh1, 1.0, 0.0), axis=0, keepdims=True)
    base = base_ref[...]
    rank0 = jnp.sum(jnp.where(oh0, before[:, :ROUTER_LANES] + base, 0.0), axis=-1, keepdims=True)
    rank1 = jnp.sum(jnp.where(oh1, before[:, ROUTER_LANES:] + base + cnt0, 0.0), axis=-1, keepdims=True)
    new_base = base + cnt0 + cnt1
    base_ref[...] = new_base
    cnt_ref[...] = new_base

    rt = jnp.where(lane == 0, i0 - EXPERT_LANE0, 0.0)
    rt = jnp.where(lane == 1, i1 - EXPERT_LANE0, rt)
    rt = jnp.where(lane == 2, rank0, rt)
    rt = jnp.where(lane == 3, rank1, rt)
    rt = jnp.where(lane == 4, w0, rt)
    rt = jnp.where(lane == 5, w1, rt)
    rt_ref[...] = rt


def _mix_route(x2d, attn2d, cb, u, ada3, conv_w, conv_b, woa, woc, g1, b1, wr, br, seq):
    t = x2d.shape[0]
    tm = TM_MIX
    tiles_per_seq = seq // tm
    hb = tm // BF16_SUBLANES
    n_halo = t // BF16_SUBLANES
    const = lambda i: (0, 0)
    return pl.pallas_call(
        functools.partial(_mix_route_kernel, seq=seq),
        out_shape=(
            jax.ShapeDtypeStruct((t, D_MODEL), F32),
            jax.ShapeDtypeStruct((t, HALF), U32),
            jax.ShapeDtypeStruct((t, ROUTER_LANES), F32),
            jax.ShapeDtypeStruct((1, ROUTER_LANES), F32),
        ),
        grid=(t // tm,),
        in_specs=[
            pl.BlockSpec((tm, D_MODEL), lambda i: (i, 0)),
            pl.BlockSpec((tm, ATTN_WIDTH), lambda i: (i, 0)),
            pl.BlockSpec((tm, CONV_WIDTH), lambda i: (i, 0)),
            pl.BlockSpec((tm, CONV_WIDTH), lambda i: (i, 0)),
            pl.BlockSpec((BF16_SUBLANES, CONV_WIDTH), lambda i: (jnp.maximum(i * hb - 1, 0), 0)),
            pl.BlockSpec((BF16_SUBLANES, CONV_WIDTH), lambda i: (jnp.minimum((i + 1) * hb, n_halo - 1), 0)),
            pl.BlockSpec((1, 6, D_MODEL), lambda i: (i // tiles_per_seq, 0, 0)),
            pl.BlockSpec((CONV_K, CONV_WIDTH), const),
            pl.BlockSpec((1, CONV_WIDTH), const),
            pl.BlockSpec((ATTN_WIDTH, D_MODEL), const),
            pl.BlockSpec((CONV_WIDTH, D_MODEL), const),
            pl.BlockSpec((1, D_MODEL), const),
            pl.BlockSpec((1, D_MODEL), const),
            pl.BlockSpec((D_MODEL, 2 * ROUTER_LANES), const),
            pl.BlockSpec((1, ROUTER_LANES), const),
        ],
        out_specs=(
            pl.BlockSpec((tm, D_MODEL), lambda i: (i, 0)),
            pl.BlockSpec((tm, HALF), lambda i: (i, 0)),
            pl.BlockSpec((tm, ROUTER_LANES), lambda i: (i, 0)),
            pl.BlockSpec((1, ROUTER_LANES), const),
        ),
        scratch_shapes=[pltpu.VMEM((1, ROUTER_LANES), F32)],
        compiler_params=_cparams(("arbitrary",)),
        name="mix_route",
    )(x2d, attn2d, cb, u, u, u, ada3, conv_w, conv_b, woa, woc, g1, b1, wr, br)


def _row_copy(src, src_row, dst, dst_row, sem):
    return pltpu.make_async_copy(src.at[pl.ds(src_row, 1), :], dst.at[pl.ds(dst_row, 1), :], sem)


def _dispatch_kernel(dest_ref, h2_ref, xs_in_ref, xs_ref, sem):
    del xs_in_ref
    tm = TM_ROW

    def body(t, c):
        _row_copy(h2_ref, t, xs_ref, dest_ref[0, 0, t], sem).start()
        _row_copy(h2_ref, t, xs_ref, dest_ref[0, 0, tm + t], sem).start()
        return c

    lax.fori_loop(0, tm, body, 0, unroll=8)
    for _ in range(2):
        pltpu.make_async_copy(h2_ref, xs_ref.at[pl.ds(0, tm), :], sem).wait()


def _dispatch(dest_tiles, h2, xs_zero):
    t = h2.shape[0]
    tm = TM_ROW
    return pl.pallas_call(
        _dispatch_kernel,
        out_shape=jax.ShapeDtypeStruct(xs_zero.shape, U32),
        grid=(t // tm,),
        in_specs=[
            pl.BlockSpec((1, 1, 2 * tm), lambda i: (i, 0, 0), memory_space=pltpu.SMEM),
            pl.BlockSpec((tm, HALF), lambda i: (i, 0)),
            pl.BlockSpec(memory_space=pl.ANY),
        ],
        out_specs=pl.BlockSpec(memory_space=pl.ANY),
        scratch_shapes=[pltpu.SemaphoreType.DMA],
        input_output_aliases={2: 0},
        compiler_params=_cparams(("arbitrary",)),
        name="dispatch",
    )(dest_tiles, h2, xs_zero)


def _experts_kernel(blk_exp_ref, n_used_ref, xs_ref, wg_ref, wu_ref, wd_ref, y_ref):
    del blk_exp_ref
    used = pl.program_id(0) < n_used_ref[0]

    @pl.when(jnp.logical_not(used))
    def _():
        y_ref[...] = jnp.zeros_like(y_ref)

    @pl.when(used)
    def _():
        x_lo, x_hi = _unpack_rows(xs_ref[...])
        x_lo = x_lo.astype(BF16)
        x_hi = x_hi.astype(BF16)
        g = _dot(x_lo, wg_ref[0, :HALF, :]) + _dot(x_hi, wg_ref[0, HALF:, :])
        up = _dot(x_lo, wu_ref[0, :HALF, :]) + _dot(x_hi, wu_ref[0, HALF:, :])
        act = (g * jax.nn.sigmoid(g) * up).astype(BF16)
        y_ref[...] = _pack_rows(_dot(act, wd_ref[0]))


def _experts(blk_exp, n_used, xs, wg, wu, wd):
    rows = xs.shape[0]
    blk = EXPERT_BLOCK
    return pl.pallas_call(
        _experts_kernel,
        out_shape=jax.ShapeDtypeStruct((rows, HALF), U32),
        grid_spec=pltpu.PrefetchScalarGridSpec(
            num_scalar_prefetch=2,
            grid=(rows // blk,),
            in_specs=[
                pl.BlockSpec((blk, HALF), lambda i, be, nu: (i, 0)),
                pl.BlockSpec((1, D_MODEL, D_FF_EXPERT), lambda i, be, nu: (be[i], 0, 0)),
                pl.BlockSpec((1, D_MODEL, D_FF_EXPERT), lambda i, be, nu: (be[i], 0, 0)),
                pl.BlockSpec((1, D_FF_EXPERT, D_MODEL), lambda i, be, nu: (be[i], 0, 0)),
            ],
            out_specs=pl.BlockSpec((blk, HALF), lambda i, be, nu: (i, 0)),
        ),
        compiler_params=_cparams(("arbitrary",)),
        name="experts",
    )(blk_exp, n_used, xs, wg, wu, wd)


def _combine_kernel(dest_ref, x1_ref, rt_ref, ada_ref, g2_ref, b2_ref, y_ref, o_ref, ybuf, sem):
    tm = TM_ROW

    def body(t, c):
        _row_copy(y_ref, dest_ref[0, 0, t], ybuf.at[0], t, sem).start()
        _row_copy(y_ref, dest_ref[0, 0, tm + t], ybuf.at[1], t, sem).start()
        return c

    lax.fori_loop(0, tm, body, 0, unroll=8)
    for k in range(2):
        pltpu.make_async_copy(y_ref.at[pl.ds(0, tm), :], ybuf.at[k], sem).wait()

    g_f = ada_ref[0][5:6, :]
    rt = rt_ref[...]
    w0 = rt[:, 4:5]
    w1 = rt[:, 5:6]
    y0_lo, y0_hi = _unpack_rows(ybuf[0])
    y1_lo, y1_hi = _unpack_rows(ybuf[1])
    f = jnp.concatenate([y0_lo * w0 + y1_lo * w1, y0_hi * w0 + y1_hi * w1], axis=1)
    z = DEEPNORM_ALPHA * x1_ref[...] + g_f * f
    o_ref[...] = _layernorm(z) * g2_ref[...] + b2_ref[...]


def _combine(dest_tiles, x1, rt, ada3, g2, b2, y, seq):
    t = x1.shape[0]
    tm = TM_ROW
    tiles_per_seq = seq // tm
    const = lambda i: (0, 0)
    return pl.pallas_call(
        _combine_kernel,
        out_shape=jax.ShapeDtypeStruct((t, D_MODEL), F32),
        grid=(t // tm,),
        in_specs=[
            pl.BlockSpec((1, 1, 2 * tm), lambda i: (i, 0, 0), memory_space=pltpu.SMEM),
            pl.BlockSpec((tm, D_MODEL), lambda i: (i, 0)),
            pl.BlockSpec((tm, ROUTER_LANES), lambda i: (i, 0)),
            pl.BlockSpec((1, 6, D_MODEL), lambda i: (i // tiles_per_seq, 0, 0)),
            pl.BlockSpec((1, D_MODEL), const),
            pl.BlockSpec((1, D_MODEL), const),
            pl.BlockSpec(memory_space=pl.ANY),
        ],
        out_specs=pl.BlockSpec((tm, D_MODEL), lambda i: (i, 0)),
        scratch_shapes=[pltpu.VMEM((2, tm, HALF), U32), pltpu.SemaphoreType.DMA],
        compiler_params=_cparams(("arbitrary",)),
        name="combine",
    )(dest_tiles, x1, rt, ada3, g2, b2, y)


def _rope_tables(seq):
    half = QK_HEAD_DIM // 2
    inv = 1.0 / (ROPE_THETA ** (jnp.arange(0, QK_HEAD_DIM, 2, dtype=F32) / QK_HEAD_DIM))
    ang = jnp.arange(seq, dtype=F32)[:, None] * inv[None, :]
    cos, sin = jnp.cos(ang), jnp.sin(ang)
    cos_t = jnp.tile(cos, (1, LANES // half))
    sin_t = jnp.tile(jnp.concatenate([-sin, sin], axis=1), (1, LANES // QK_HEAD_DIM))
    return cos_t, sin_t


def _routing_tables(rt, counts, n_rows_max):
    t = rt.shape[0]
    blk = EXPERT_BLOCK
    counts = counts[0, EXPERT_LANE0:EXPERT_LANE0 + N_EXPERTS].astype(jnp.int32)
    padded = ((counts + blk - 1) // blk) * blk
    pend = jnp.cumsum(padded)
    pstart = pend - padded
    e = rt[:, 0:2].astype(jnp.int32)
    rank = rt[:, 2:4].astype(jnp.int32)
    dest = pstart[e] + rank
    dest_tiles = dest.reshape(t // TM_ROW, TM_ROW, 2).transpose(0, 2, 1).reshape(t // TM_ROW, 1, 2 * TM_ROW)
    blk_start = jnp.arange(n_rows_max // blk, dtype=jnp.int32) * blk
    blk_exp = jnp.minimum(jnp.sum(blk_start[:, None] >= pend[None, :], axis=1), N_EXPERTS - 1).astype(jnp.int32)
    n_used = (pend[-1:] // blk).astype(jnp.int32)
    return dest_tiles, blk_exp, n_used


def _encoder(x, c, p):
    b, seq, d = x.shape
    t = b * seq
    x2d = x.reshape(t, d)
    ada3 = _ada(c, p["w_ada_hi"], p["w_ada_lo"], p["b_ada"]).reshape(b, 6, d)
    cos_t, sin_t = _rope_tables(seq)
    q, k, v, cb, u = _in_proj(x2d, ada3, p["w_in"], cos_t, sin_t, seq)
    attn = _diff_attn(q.reshape(b, seq, 512), k.reshape(b, seq, 512), v.reshape(b, seq, 512),
                      p["lq1"], p["lk1"], p["lq2"], p["lk2"], p["subln_g"])
    x1, h2, rt, counts = _mix_route(x2d, attn.reshape(t, 512), cb, u, ada3, p["conv_w"], p["conv_b"],
                                    p["woa"], p["woc"], p["ln1_g"], p["ln1_b"], p["wr"], p["br"], seq)
    n_rows_max = (2 * t // EXPERT_BLOCK + N_EXPERTS) * EXPERT_BLOCK
    dest_tiles, blk_exp, n_used = _routing_tables(rt, counts, n_rows_max)
    xs = _dispatch(dest_tiles, h2, jnp.zeros((n_rows_max, HALF), U32))
    y = _experts(blk_exp, n_used, xs, p["wg"], p["wu"], p["wd"])
    out = _combine(dest_tiles, x1, rt, ada3, p["ln2_g"], p["ln2_b"], y, seq)
    return out.reshape(b, seq, d)


def kernel(x_prompt, x_sample, c_prompt, c_sample, w_ada, b_ada, w_in, lambda_q1, lambda_k1, lambda_q2, lambda_k2, attn_subln_g, conv_w, conv_b, w_out, ln1_g, ln1_b, router_group_w, router_group_b, router_expert_w, router_expert_b, expert_w_gate, expert_w_up, expert_w_down, ln2_g, ln2_b):
    l = 0
    w_ada_hi, w_ada_lo = _split_hi_lo(w_ada[l])
    wr = jnp.concatenate([router_group_w[l], router_expert_w[l]], axis=1)
    wr = jnp.pad(wr, ((0, 0), (0, ROUTER_LANES - wr.shape[1])))
    wr_hi, wr_lo = _split_hi_lo(wr)
    br = jnp.concatenate([router_group_b[l], router_expert_b[l]])
    br = jnp.pad(br, (0, ROUTER_LANES - br.shape[0])).reshape(1, ROUTER_LANES)
    w_out_bf = w_out[l].astype(BF16)
    p = {
        "w_ada_hi": w_ada_hi, "w_ada_lo": w_ada_lo, "b_ada": b_ada[l].reshape(1, -1),
        "w_in": w_in[l].astype(BF16),
        "lq1": lambda_q1[l].reshape(1, -1), "lk1": lambda_k1[l].reshape(1, -1),
        "lq2": lambda_q2[l].reshape(1, -1), "lk2": lambda_k2[l].reshape(1, -1),
        "subln_g": attn_subln_g[l].reshape(1, -1),
        "conv_w": conv_w[l], "conv_b": conv_b[l].reshape(1, -1),
        "woa": w_out_bf[:ATTN_WIDTH], "woc": w_out_bf[ATTN_WIDTH:],
        "ln1_g": ln1_g[l].reshape(1, -1), "ln1_b": ln1_b[l].reshape(1, -1),
        "wr": jnp.concatenate([wr_hi, wr_lo], axis=1), "br": br,
        "wg": expert_w_gate[l].astype(BF16), "wu": expert_w_up[l].astype(BF16),
        "wd": expert_w_down[l].astype(BF16),
        "ln2_g": ln2_g[l].reshape(1, -1), "ln2_b": ln2_b[l].reshape(1, -1),
    }
    y_prompt = _encoder(x_prompt, c_prompt, p)
    y_sample = _encoder(x_sample, c_sample, p)
    return (y_prompt, y_sample)
```

```python
import functools
import math

import jax
import jax.numpy as jnp
from jax import lax
from jax.experimental import pallas as pl
from jax.experimental.pallas import tpu as pltpu
from jax.experimental.pallas import tpu_sc as plsc

D_MODEL = 1024
ATTN_WIDTH = 512
CONV_WIDTH = 512
QK_HEAD_DIM = 64
V_HEAD_DIM = 128
N_DIFF_HEADS = 4
IN_WIDTH = 3072
CONV_K = 3
ROPE_THETA = 10000.0
N_GROUPS = 4
EXPERTS_PER_GROUP = 8
N_EXPERTS = 32
D_FF_EXPERT = 512
LN_EPS = 1e-5
RMS_EPS = 1e-5
DEPTH = 1
DEEPNORM_ALPHA = (2.0 * DEPTH) ** 0.25
LAMBDA_INIT = 0.8 - 0.6 * math.exp(-0.3 * 0)

LANES = 128
BF16_SUBLANES = 16
VMEM_LIMIT = 48 * 1024 * 1024

TM_PROJ = 512
TQ = 512
TK = 512
TM_MIX = 512
TM_COMBINE = 512
SC_WINDOW = 256
EXPERT_BLOCK = 512
ROUTER_LANES = 128
EXPERT_LANE0 = N_GROUPS

BF16 = jnp.bfloat16
F32 = jnp.float32


def _cparams(sem):
    return pltpu.CompilerParams(dimension_semantics=sem, vmem_limit_bytes=VMEM_LIMIT)


def _layernorm(x):
    mu = jnp.mean(x, axis=-1, keepdims=True)
    xc = x - mu
    var = jnp.mean(xc * xc, axis=-1, keepdims=True)
    return xc * lax.rsqrt(var + LN_EPS)


def _split_hi_lo(a):
    hi = a.astype(BF16)
    lo = (a - hi.astype(F32)).astype(BF16)
    return hi, lo


def _dot(a, b):
    return jnp.dot(a, b, preferred_element_type=F32)


HALF = D_MODEL // 2
U32 = jnp.uint32
PLANES = HALF // LANES


def _pack_rows(x):
    lo = lax.bitcast_convert_type(x[:, :HALF].astype(BF16).astype(F32), U32)
    hi = lax.bitcast_convert_type(x[:, HALF:].astype(BF16).astype(F32), U32)
    return (lo >> 16) | hi


def _unpack_rows(w):
    lo = lax.bitcast_convert_type(w << 16, F32)
    hi = lax.bitcast_convert_type(w & jnp.uint32(0xFFFF0000), F32)
    return lo, hi


def _ada_kernel(c_ref, whi_ref, wlo_ref, b_ref, o_ref):
    c = c_ref[...]
    s = c * jax.nn.sigmoid(c)
    s_hi, s_lo = _split_hi_lo(s)
    acc = _dot(s_hi, whi_ref[...]) + _dot(s_lo, whi_ref[...]) + _dot(s_hi, wlo_ref[...])
    o_ref[...] = acc + b_ref[...]


def _ada(c, w_hi, w_lo, b):
    bsz = c.shape[0]
    n = w_hi.shape[1]
    tn = 1024
    return pl.pallas_call(
        _ada_kernel,
        out_shape=jax.ShapeDtypeStruct((bsz, n), F32),
        grid=(n // tn,),
        in_specs=[
            pl.BlockSpec((bsz, D_MODEL), lambda j: (0, 0)),
            pl.BlockSpec((D_MODEL, tn), lambda j: (0, j)),
            pl.BlockSpec((D_MODEL, tn), lambda j: (0, j)),
            pl.BlockSpec((1, tn), lambda j: (0, j)),
        ],
        out_specs=pl.BlockSpec((bsz, tn), lambda j: (0, j)),
        compiler_params=_cparams(("arbitrary",)),
        name="ada",
    )(c, w_hi, w_lo, b)


def _rope(x, cos_t, sin_t):
    lane = lax.broadcasted_iota(jnp.int32, x.shape, 1)
    upper = (lane & 32) != 0
    partner = jnp.where(upper, pltpu.roll(x, 32, axis=1), pltpu.roll(x, LANES - 32, axis=1))
    return x * cos_t + partner * sin_t


def _in_proj_kernel(x_ref, ada_ref, w_ref, cos_ref, sin_ref,
                    q_ref, k_ref, v_ref, cb_ref, u_ref):
    x = x_ref[...]
    ada = ada_ref[0]
    sh_m = ada[0:1, :]
    sc_m = ada[1:2, :]
    h = (_layernorm(x) * (1.0 + sc_m) + sh_m).astype(BF16)
    cos_t = cos_ref[...]
    sin_t = sin_ref[...]
    qk_scale = QK_HEAD_DIM ** -0.5 * math.log2(math.e)
    for j in range(4):
        lo = j * LANES
        qj = _dot(h, w_ref[:, lo:lo + LANES])
        q_ref[:, lo:lo + LANES] = (_rope(qj, cos_t, sin_t) * qk_scale).astype(BF16)
        kj = _dot(h, w_ref[:, 512 + lo:512 + lo + LANES])
        k_ref[:, lo:lo + LANES] = _rope(kj, cos_t, sin_t).astype(BF16)
    v_ref[...] = _dot(h, w_ref[:, 1024:1536]).astype(BF16)
    cb_ref[...] = _dot(h, w_ref[:, 1536:2048]).astype(BF16)
    cc = _dot(h, w_ref[:, 2048:2560])
    ch = _dot(h, w_ref[:, 2560:3072])
    u_ref[...] = (cc * ch).astype(BF16)


def _in_proj(x2d, ada3, w_in_bf, cos_t, sin_t, seq):
    t = x2d.shape[0]
    tm = TM_PROJ
    tiles_per_seq = seq // tm
    out = jax.ShapeDtypeStruct((t, 512), BF16)
    ospec = pl.BlockSpec((tm, 512), lambda i: (i, 0))
    return pl.pallas_call(
        _in_proj_kernel,
        out_shape=(out,) * 5,
        grid=(t // tm,),
        in_specs=[
            pl.BlockSpec((tm, D_MODEL), lambda i: (i, 0)),
            pl.BlockSpec((1, 6, D_MODEL), lambda i: (i // tiles_per_seq, 0, 0)),
            pl.BlockSpec((D_MODEL, IN_WIDTH), lambda i: (0, 0)),
            pl.BlockSpec((tm, LANES), lambda i: (i % tiles_per_seq, 0)),
            pl.BlockSpec((tm, LANES), lambda i: (i % tiles_per_seq, 0)),
        ],
        out_specs=(ospec,) * 5,
        compiler_params=_cparams(("arbitrary",)),
        name="in_proj",
    )(x2d, ada3, w_in_bf, cos_t, sin_t)


def _diff_attn_kernel(q_ref, k_ref, v_ref, lq1_ref, lk1_ref, lq2_ref, lk2_ref, g_ref, o_ref,
                      qq_ref, vx_ref, s_buf, m_ref, acc_ref, *, seq):
    q = q_ref[0]
    lane = lax.broadcasted_iota(jnp.int32, q.shape, 1)
    zero = jnp.zeros_like(q)
    qq_ref[0:TQ, :] = jnp.where(lane < QK_HEAD_DIM, q, zero)
    qq_ref[TQ:2 * TQ, :] = jnp.where(lane >= QK_HEAD_DIM, q, zero)
    m_ref[...] = jnp.full(m_ref.shape, -jnp.inf, F32)
    acc_ref[...] = jnp.zeros(acc_ref.shape, F32)
    nk = seq // TK
    n_lane_blocks = TK // LANES

    @pl.when(pl.program_id(2) == 0)
    def _():
        vx_ref[:, 0:V_HEAD_DIM] = v_ref[0]
        vx_ref[:, V_HEAD_DIM:2 * V_HEAD_DIM] = jnp.ones((seq, V_HEAD_DIM), BF16)

    def scores(j, slot):
        start = pl.multiple_of(j * TK, TK)
        kc = k_ref[0, pl.ds(start, TK), :]
        s_buf[slot] = lax.dot_general(qq_ref[...], kc, (((1,), (1,)), ((), ())), preferred_element_type=F32)

    def softmax_pv(j, slot):
        blocks = [s_buf[slot, :, c * LANES:(c + 1) * LANES] for c in range(n_lane_blocks)]
        mb = blocks[0]
        for c in range(1, n_lane_blocks):
            mb = jnp.maximum(mb, blocks[c])
        m_old = m_ref[...]
        m_new = jnp.maximum(m_old, jnp.max(mb, axis=-1, keepdims=True))
        alpha = jnp.exp2(m_old - m_new)
        m_ref[...] = m_new
        p = jnp.concatenate([jnp.exp2(blk - m_new).astype(BF16) for blk in blocks], axis=1)
        start = pl.multiple_of(j * TK, TK)
        pv = _dot(p, vx_ref[pl.ds(start, TK), :])
        acc_ref[:, 0:V_HEAD_DIM] = alpha * acc_ref[:, 0:V_HEAD_DIM] + pv[:, 0:V_HEAD_DIM]
        acc_ref[:, V_HEAD_DIM:] = alpha * acc_ref[:, V_HEAD_DIM:] + pv[:, V_HEAD_DIM:]

    scores(0, 0)

    def pair(jj, c):
        j = 2 * jj
        scores(j + 1, 1)
        softmax_pv(j, 0)
        scores(j + 2, 0)
        softmax_pv(j + 1, 1)
        return c

    lax.fori_loop(0, nk // 2 - 1, pair, 0)
    scores(nk - 1, 1)
    softmax_pv(nk - 2, 0)
    softmax_pv(nk - 1, 1)

    o = acc_ref[:, 0:V_HEAD_DIM] / acc_ref[:, V_HEAD_DIM:]
    lam = (jnp.exp(jnp.sum(lq1_ref[...] * lk1_ref[...], axis=-1, keepdims=True))
           - jnp.exp(jnp.sum(lq2_ref[...] * lk2_ref[...], axis=-1, keepdims=True)) + LAMBDA_INIT)
    of = o[:TQ] - lam * o[TQ:]
    of = of * lax.rsqrt(jnp.mean(of * of, axis=-1, keepdims=True) + RMS_EPS)
    of = of * g_ref[...] * (1.0 - LAMBDA_INIT)
    o_ref[0] = of.astype(BF16)


def _diff_attn(q, k, v, lq1, lk1, lq2, lk2, g):
    b, seq, _ = q.shape
    lam_spec = pl.BlockSpec((1, QK_HEAD_DIM), lambda bi, h, qi: (0, 0))
    return pl.pallas_call(
        functools.partial(_diff_attn_kernel, seq=seq),
        out_shape=jax.ShapeDtypeStruct((b, seq, ATTN_WIDTH), BF16),
        grid=(b, N_DIFF_HEADS, seq // TQ),
        in_specs=[
            pl.BlockSpec((1, TQ, LANES), lambda bi, h, qi: (bi, qi, h)),
            pl.BlockSpec((1, seq, LANES), lambda bi, h, qi: (bi, 0, h)),
            pl.BlockSpec((1, seq, LANES), lambda bi, h, qi: (bi, 0, h)),
            lam_spec, lam_spec, lam_spec, lam_spec,
            pl.BlockSpec((1, V_HEAD_DIM), lambda bi, h, qi: (0, 0)),
        ],
        out_specs=pl.BlockSpec((1, TQ, LANES), lambda bi, h, qi: (bi, qi, h)),
        scratch_shapes=[
            pltpu.VMEM((2 * TQ, LANES), BF16),
            pltpu.VMEM((seq, 2 * V_HEAD_DIM), BF16),
            pltpu.VMEM((2, 2 * TQ, TK), F32),
            pltpu.VMEM((2 * TQ, LANES), F32),
            pltpu.VMEM((2 * TQ, 2 * V_HEAD_DIM), F32),
        ],
        compiler_params=_cparams(("arbitrary", "arbitrary", "arbitrary")),
        name="diff_attn",
    )(q, k, v, lq1, lk1, lq2, lk2, g)


def _lane_min_index(mask, lane_f):
    return jnp.min(jnp.where(mask, lane_f, float(ROUTER_LANES)), axis=-1, keepdims=True)


def _mix_route_kernel(x_ref, attn_ref, cb_ref, u_ref, uprev_ref, unext_ref, ada_ref,
                      cw_ref, cbias_ref, woa_ref, woc_ref, g1_ref, b1_ref, wr_ref, br_ref,
                      x1_ref, h2_ref, rt_ref, cnt_ref, base_ref, *, seq):
    i = pl.program_id(0)
    tm = TM_MIX

    @pl.when(i == 0)
    def _():
        base_ref[...] = jnp.zeros_like(base_ref)

    ada = ada_ref[0]
    g_m = ada[2:3, :]
    sh_f = ada[3:4, :]
    sc_f = ada[4:5, :]

    u = u_ref[...].astype(F32)
    row = lax.broadcasted_iota(jnp.int32, u.shape, 0)
    not_seq_start = ((i * tm) % seq != 0).astype(F32)
    not_seq_end = (((i + 1) * tm) % seq != 0).astype(F32)
    halo_prev = uprev_ref[...].astype(F32)[BF16_SUBLANES - 1:BF16_SUBLANES, :] * not_seq_start
    halo_next = unext_ref[...].astype(F32)[0:1, :] * not_seq_end
    u_prev = jnp.where(row == 0, halo_prev, pltpu.roll(u, 1, axis=0))
    u_next = jnp.where(row == tm - 1, halo_next, pltpu.roll(u, tm - 1, axis=0))
    cw = cw_ref[...]
    y = cbias_ref[...] + u_prev * cw[0:1, :]
    y = y + u * cw[1:2, :]
    y = y + u_next * cw[2:3, :]
    conv = (cb_ref[...].astype(F32) * y).astype(BF16)

    mix = _dot(attn_ref[...], woa_ref[...]) + _dot(conv, woc_ref[...])
    x1 = _layernorm(DEEPNORM_ALPHA * x_ref[...] + g_m * mix) * g1_ref[...] + b1_ref[...]
    x1_ref[...] = x1
    h2 = _layernorm(x1) * (1.0 + sc_f) + sh_f
    for c, plane in enumerate(_to_planes(_pack_rows(h2))):
        h2_ref[c] = plane

    h_hi, h_lo = _split_hi_lo(h2)
    both = _dot(h_hi, wr_ref[...])
    logits = (both[:, :ROUTER_LANES] + both[:, ROUTER_LANES:]
              + _dot(h_lo, wr_ref[:, :ROUTER_LANES]) + br_ref[...])

    lane = lax.broadcasted_iota(jnp.int32, logits.shape, 1)
    lane_f = lane.astype(F32)
    neg = -jnp.inf
    is_group = lane < N_GROUPS
    lg = jnp.where(is_group, logits, neg)
    lg_max = jnp.max(lg, axis=-1, keepdims=True)
    g_sel = _lane_min_index(lg == lg_max, lane_f)
    pg_sel = 1.0 / jnp.sum(jnp.exp(lg - lg_max), axis=-1, keepdims=True)

    first = EXPERT_LANE0 + EXPERTS_PER_GROUP * g_sel
    in_group = (lane_f >= first) & (lane_f < first + EXPERTS_PER_GROUP)
    le = jnp.where(in_group, logits, neg)
    l0 = jnp.max(le, axis=-1, keepdims=True)
    i0 = _lane_min_index(le == l0, lane_f)
    le2 = jnp.where(lane_f == i0, neg, le)
    l1 = jnp.max(le2, axis=-1, keepdims=True)
    i1 = _lane_min_index(le2 == l1, lane_f)
    t_exp = jnp.exp(l1 - l0)
    p0 = 1.0 / (1.0 + t_exp)
    w0 = pg_sel * p0
    w1 = pg_sel * (t_exp * p0)

    oh0 = lane_f == i0
    oh1 = lane_f == i1
    onehots = jnp.concatenate([jnp.where(oh0, 1.0, 0.0), jnp.where(oh1, 1.0, 0.0)], axis=1).astype(BF16)
    r_i = lax.broadcasted_iota(jnp.int32, (tm, tm), 0)
    c_i = lax.broadcasted_iota(jnp.int32, (tm, tm), 1)
    tri = jnp.where(c_i < r_i, 1.0, 0.0).astype(BF16)
    before = _dot(tri, onehots)
    cnt0 = jnp.sum(jnp.where(oh0, 1.0, 0.0), axis=0, keepdims=True)
    cnt1 = jnp.sum(jnp.where(oh1, 1.0, 0.0), axis=0, keepdims=True)
    base = base_ref[...]
    rank0 = jnp.sum(jnp.where(oh0, before[:, :ROUTER_LANES] + base, 0.0), axis=-1, keepdims=True)
    rank1 = jnp.sum(jnp.where(oh1, before[:, ROUTER_LANES:] + base + cnt0, 0.0), axis=-1, keepdims=True)
    new_base = base + cnt0 + cnt1
    base_ref[...] = new_base
    cnt_ref[...] = new_base

    rt = jnp.where(lane == 0, i0 - EXPERT_LANE0, 0.0)
    rt = jnp.where(lane == 1, i1 - EXPERT_LANE0, rt)
    rt = jnp.where(lane == 2, rank0, rt)
    rt = jnp.where(lane == 3, rank1, rt)
    rt = jnp.where(lane == 4, w0, rt)
    rt = jnp.where(lane == 5, w1, rt)
    rt_ref[...] = rt


def _mix_route(x2d, attn2d, cb, u, ada3, conv_w, conv_b, woa, woc, g1, b1, wr, br, seq):
    t = x2d.shape[0]
    tm = TM_MIX
    tiles_per_seq = seq // tm
    hb = tm // BF16_SUBLANES
    n_halo = t // BF16_SUBLANES
    const = lambda i: (0, 0)
    return pl.pallas_call(
        functools.partial(_mix_route_kernel, seq=seq),
        out_shape=(
            jax.ShapeDtypeStruct((t, D_MODEL), F32),
            jax.ShapeDtypeStruct((PLANES, t, LANES), U32),
            jax.ShapeDtypeStruct((t, ROUTER_LANES), F32),
            jax.ShapeDtypeStruct((1, ROUTER_LANES), F32),
        ),
        grid=(t // tm,),
        in_specs=[
            pl.BlockSpec((tm, D_MODEL), lambda i: (i, 0)),
            pl.BlockSpec((tm, ATTN_WIDTH), lambda i: (i, 0)),
            pl.BlockSpec((tm, CONV_WIDTH), lambda i: (i, 0)),
            pl.BlockSpec((tm, CONV_WIDTH), lambda i: (i, 0)),
            pl.BlockSpec((BF16_SUBLANES, CONV_WIDTH), lambda i: (jnp.maximum(i * hb - 1, 0), 0)),
            pl.BlockSpec((BF16_SUBLANES, CONV_WIDTH), lambda i: (jnp.minimum((i + 1) * hb, n_halo - 1), 0)),
            pl.BlockSpec((1, 6, D_MODEL), lambda i: (i // tiles_per_seq, 0, 0)),
            pl.BlockSpec((CONV_K, CONV_WIDTH), const),
            pl.BlockSpec((1, CONV_WIDTH), const),
            pl.BlockSpec((ATTN_WIDTH, D_MODEL), const),
            pl.BlockSpec((CONV_WIDTH, D_MODEL), const),
            pl.BlockSpec((1, D_MODEL), const),
            pl.BlockSpec((1, D_MODEL), const),
            pl.BlockSpec((D_MODEL, 2 * ROUTER_LANES), const),
            pl.BlockSpec((1, ROUTER_LANES), const),
        ],
        out_specs=(
            pl.BlockSpec((tm, D_MODEL), lambda i: (i, 0)),
            pl.BlockSpec((PLANES, tm, LANES), lambda i: (0, i, 0)),
            pl.BlockSpec((tm, ROUTER_LANES), lambda i: (i, 0)),
            pl.BlockSpec((1, ROUTER_LANES), const),
        ),
        scratch_shapes=[pltpu.VMEM((1, ROUTER_LANES), F32)],
        compiler_params=_cparams(("arbitrary",)),
        name="mix_route",
    )(x2d, attn2d, cb, u, u, u, ada3, conv_w, conv_b, woa, woc, g1, b1, wr, br)


def _sc_mesh():
    return plsc.VectorSubcoreMesh(core_axis_name="core", subcore_axis_name="subcore")


def _sc_scatter_rows(src, idx, n_dst):
    m = idx.shape[0]
    w = SC_WINDOW
    n_src_blocks = src.shape[0] // w

    @pl.kernel(out_type=jax.ShapeDtypeStruct((n_dst, LANES), src.dtype), mesh=_sc_mesh(), scratch_types=[])
    def scatter(x_hbm, i_hbm, o_hbm):
        def body(x_vmem, i_vmem):
            pltpu.sync_copy(x_vmem, o_hbm.at[i_vmem.at[0]])

        pltpu.emit_pipeline(
            body, grid=(m // w,),
            in_specs=[pl.BlockSpec((w, LANES), index_map=lambda i: (i % n_src_blocks, 0)),
                      pl.BlockSpec((1, w), index_map=lambda i: (0, i))],
            out_specs=[],
            core_axis_name=("core", "subcore"),
            dimension_semantics=(pltpu.PARALLEL,),
        )(x_hbm, i_hbm)

    return scatter(src, idx.reshape(1, m))


def _sc_gather_rows(src, idx):
    m = idx.shape[0]
    w = SC_WINDOW

    @pl.kernel(out_type=jax.ShapeDtypeStruct((m, LANES), src.dtype), mesh=_sc_mesh(), scratch_types=[])
    def gather(x_hbm, i_hbm, o_hbm):
        def body(i_vmem, o_vmem):
            pltpu.sync_copy(x_hbm.at[i_vmem.at[0]], o_vmem)

        pltpu.emit_pipeline(
            body, grid=(m // w,),
            in_specs=[pl.BlockSpec((1, w), index_map=lambda i: (0, i))],
            out_specs=[pl.BlockSpec((w, LANES), index_map=lambda i: (i, 0))],
            core_axis_name=("core", "subcore"),
            dimension_semantics=(pltpu.PARALLEL,),
        )(i_hbm, o_hbm)

    return gather(src, idx.reshape(1, m))


def _to_planes(words):
    return [words[:, c * LANES:(c + 1) * LANES] for c in range(PLANES)]


def _experts_kernel(blk_exp_ref, n_used_ref, xs_ref, wg_ref, wu_ref, wd_ref, y_ref):
    del blk_exp_ref
    used = pl.program_id(0) < n_used_ref[0]

    @pl.when(jnp.logical_not(used))
    def _():
        y_ref[...] = jnp.zeros_like(y_ref)

    @pl.when(used)
    def _():
        x_lo, x_hi = _unpack_rows(jnp.concatenate([xs_ref[c] for c in range(PLANES)], axis=1))
        x_lo = x_lo.astype(BF16)
        x_hi = x_hi.astype(BF16)
        g = _dot(x_lo, wg_ref[0, :HALF, :]) + _dot(x_hi, wg_ref[0, HALF:, :])
        up = _dot(x_lo, wu_ref[0, :HALF, :]) + _dot(x_hi, wu_ref[0, HALF:, :])
        act = (g * jax.nn.sigmoid(g) * up).astype(BF16)
        for c, plane in enumerate(_to_planes(_pack_rows(_dot(act, wd_ref[0])))):
            y_ref[c] = plane


def _experts(blk_exp, n_used, xs, wg, wu, wd):
    rows = xs.shape[1]
    blk = EXPERT_BLOCK
    return pl.pallas_call(
        _experts_kernel,
        out_shape=jax.ShapeDtypeStruct((PLANES, rows, LANES), U32),
        grid_spec=pltpu.PrefetchScalarGridSpec(
            num_scalar_prefetch=2,
            grid=(rows // blk,),
            in_specs=[
                pl.BlockSpec((PLANES, blk, LANES), lambda i, be, nu: (0, i, 0)),
                pl.BlockSpec((1, D_MODEL, D_FF_EXPERT), lambda i, be, nu: (be[i], 0, 0)),
                pl.BlockSpec((1, D_MODEL, D_FF_EXPERT), lambda i, be, nu: (be[i], 0, 0)),
                pl.BlockSpec((1, D_FF_EXPERT, D_MODEL), lambda i, be, nu: (be[i], 0, 0)),
            ],
            out_specs=pl.BlockSpec((PLANES, blk, LANES), lambda i, be, nu: (0, i, 0)),
        ),
        compiler_params=_cparams(("arbitrary",)),
        name="experts",
    )(blk_exp, n_used, xs, wg, wu, wd)


def _combine_kernel(x1_ref, rt_ref, ada_ref, g2_ref, b2_ref, yg_ref, o_ref):
    g_f = ada_ref[0][5:6, :]
    rt = rt_ref[...]
    w0 = rt[:, 4:5]
    w1 = rt[:, 5:6]
    y0_lo, y0_hi = _unpack_rows(jnp.concatenate([yg_ref[0, c] for c in range(PLANES)], axis=1))
    y1_lo, y1_hi = _unpack_rows(jnp.concatenate([yg_ref[1, c] for c in range(PLANES)], axis=1))
    f = jnp.concatenate([y0_lo * w0 + y1_lo * w1, y0_hi * w0 + y1_hi * w1], axis=1)
    z = DEEPNORM_ALPHA * x1_ref[...] + g_f * f
    o_ref[...] = _layernorm(z) * g2_ref[...] + b2_ref[...]


def _combine(x1, rt, ada3, g2, b2, yg, seq):
    t = x1.shape[0]
    tm = TM_COMBINE
    tiles_per_seq = seq // tm
    const = lambda i: (0, 0)
    return pl.pallas_call(
        _combine_kernel,
        out_shape=jax.ShapeDtypeStruct((t, D_MODEL), F32),
        grid=(t // tm,),
        in_specs=[
            pl.BlockSpec((tm, D_MODEL), lambda i: (i, 0)),
            pl.BlockSpec((tm, ROUTER_LANES), lambda i: (i, 0)),
            pl.BlockSpec((1, 6, D_MODEL), lambda i: (i // tiles_per_seq, 0, 0)),
            pl.BlockSpec((1, D_MODEL), const),
            pl.BlockSpec((1, D_MODEL), const),
            pl.BlockSpec((2, PLANES, tm, LANES), lambda i: (0, 0, i, 0)),
        ],
        out_specs=pl.BlockSpec((tm, D_MODEL), lambda i: (i, 0)),
        compiler_params=_cparams(("arbitrary",)),
        name="combine",
    )(x1, rt, ada3, g2, b2, yg)


def _rope_tables(seq):
    half = QK_HEAD_DIM // 2
    inv = 1.0 / (ROPE_THETA ** (jnp.arange(0, QK_HEAD_DIM, 2, dtype=F32) / QK_HEAD_DIM))
    ang = jnp.arange(seq, dtype=F32)[:, None] * inv[None, :]
    cos, sin = jnp.cos(ang), jnp.sin(ang)
    cos_t = jnp.tile(cos, (1, LANES // half))
    sin_t = jnp.tile(jnp.concatenate([-sin, sin], axis=1), (1, LANES // QK_HEAD_DIM))
    return cos_t, sin_t


def _routing_tables(rt, counts, n_rows_max):
    t = rt.shape[0]
    blk = EXPERT_BLOCK
    counts = counts[0, EXPERT_LANE0:EXPERT_LANE0 + N_EXPERTS].astype(jnp.int32)
    padded = ((counts + blk - 1) // blk) * blk
    pend = jnp.cumsum(padded)
    pstart = pend - padded
    e = rt[:, 0:2].astype(jnp.int32)
    rank = rt[:, 2:4].astype(jnp.int32)
    dest = pstart[e] + rank
    plane0 = (jnp.arange(PLANES, dtype=jnp.int32) * n_rows_max)[None, :, None]
    row_idx = (dest.T[:, None, :] + plane0).reshape(2 * PLANES * t)
    blk_start = jnp.arange(n_rows_max // blk, dtype=jnp.int32) * blk
    blk_exp = jnp.minimum(jnp.sum(blk_start[:, None] >= pend[None, :], axis=1), N_EXPERTS - 1).astype(jnp.int32)
    n_used = (pend[-1:] // blk).astype(jnp.int32)
    return row_idx, blk_exp, n_used


def _encoder(x, c, p):
    b, seq, d = x.shape
    t = b * seq
    x2d = x.reshape(t, d)
    ada3 = _ada(c, p["w_ada_hi"], p["w_ada_lo"], p["b_ada"]).reshape(b, 6, d)
    cos_t, sin_t = _rope_tables(seq)
    q, k, v, cb, u = _in_proj(x2d, ada3, p["w_in"], cos_t, sin_t, seq)
    attn = _diff_attn(q.reshape(b, seq, 512), k.reshape(b, seq, 512), v.reshape(b, seq, 512),
                      p["lq1"], p["lk1"], p["lq2"], p["lk2"], p["subln_g"])
    x1, h2, rt, counts = _mix_route(x2d, attn.reshape(t, 512), cb, u, ada3, p["conv_w"], p["conv_b"],
                                    p["woa"], p["woc"], p["ln1_g"], p["ln1_b"], p["wr"], p["br"], seq)
    n_rows_max = (2 * t // EXPERT_BLOCK + N_EXPERTS) * EXPERT_BLOCK
    row_idx, blk_exp, n_used = _routing_tables(rt, counts, n_rows_max)
    xs = _sc_scatter_rows(h2.reshape(PLANES * t, LANES), row_idx, PLANES * n_rows_max)
    y = _experts(blk_exp, n_used, xs.reshape(PLANES, n_rows_max, LANES), p["wg"], p["wu"], p["wd"])
    yg = _sc_gather_rows(y.reshape(PLANES * n_rows_max, LANES), row_idx)
    out = _combine(x1, rt, ada3, p["ln2_g"], p["ln2_b"], yg.reshape(2, PLANES, t, LANES), seq)
    return out.reshape(b, seq, d)


def kernel(x_prompt, x_sample, c_prompt, c_sample, w_ada, b_ada, w_in, lambda_q1, lambda_k1, lambda_q2, lambda_k2, attn_subln_g, conv_w, conv_b, w_out, ln1_g, ln1_b, router_group_w, router_group_b, router_expert_w, router_expert_b, expert_w_gate, expert_w_up, expert_w_down, ln2_g, ln2_b):
    l = 0
    w_ada_hi, w_ada_lo = _split_hi_lo(w_ada[l])
    wr = jnp.concatenate([router_group_w[l], router_expert_w[l]], axis=1)
    wr = jnp.pad(wr, ((0, 0), (0, ROUTER_LANES - wr.shape[1])))
    wr_hi, wr_lo = _split_hi_lo(wr)
    br = jnp.concatenate([router_group_b[l], router_expert_b[l]])
    br = jnp.pad(br, (0, ROUTER_LANES - br.shape[0])).reshape(1, ROUTER_LANES)
    w_out_bf = w_out[l].astype(BF16)
    p = {
        "w_ada_hi": w_ada_hi, "w_ada_lo": w_ada_lo, "b_ada": b_ada[l].reshape(1, -1),
        "w_in": w_in[l].astype(BF16),
        "lq1": lambda_q1[l].reshape(1, -1), "lk1": lambda_k1[l].reshape(1, -1),
        "lq2": lambda_q2[l].reshape(1, -1), "lk2": lambda_k2[l].reshape(1, -1),
        "subln_g": attn_subln_g[l].reshape(1, -1),
        "conv_w": conv_w[l], "conv_b": conv_b[l].reshape(1, -1),
        "woa": w_out_bf[:ATTN_WIDTH], "woc": w_out_bf[ATTN_WIDTH:],
        "ln1_g": ln1_g[l].reshape(1, -1), "ln1_b": ln1_b[l].reshape(1, -1),
        "wr": jnp.concatenate([wr_hi, wr_lo], axis=1), "br": br,
        "wg": expert_w_gate[l].astype(BF16), "wu": expert_w_up[l].astype(BF16),
        "wd": expert_w_down[l].astype(BF16),
        "ln2_g": ln2_g[l].reshape(1, -1), "ln2_b": ln2_b[l].reshape(1, -1),
    }
    y_prompt = _encoder(x_prompt, c_prompt, p)
    y_sample = _encoder(x_sample, c_sample, p)
    return (y_prompt, y_sample)
```

```python
import functools
import math

import jax
import jax.numpy as jnp
from jax import lax
from jax.experimental import pallas as pl
from jax.experimental.pallas import tpu as pltpu
from jax.experimental.pallas import tpu_sc as plsc

D_MODEL = 1024
ATTN_WIDTH = 512
CONV_WIDTH = 512
QK_HEAD_DIM = 64
V_HEAD_DIM = 128
N_DIFF_HEADS = 4
IN_WIDTH = 3072
CONV_K = 3
ROPE_THETA = 10000.0
N_GROUPS = 4
EXPERTS_PER_GROUP = 8
N_EXPERTS = 32
D_FF_EXPERT = 512
LN_EPS = 1e-5
RMS_EPS = 1e-5
DEPTH = 1
DEEPNORM_ALPHA = (2.0 * DEPTH) ** 0.25
LAMBDA_INIT = 0.8 - 0.6 * math.exp(-0.3 * 0)

LANES = 128
BF16_SUBLANES = 16
VMEM_LIMIT = 48 * 1024 * 1024

TM_PROJ = 512
TQ = 512
TK = 512
TM_MIX = 512
TM_COMBINE = 512
SC_WINDOW = 256
EXPERT_BLOCK = 512
ROUTER_LANES = 128
EXPERT_LANE0 = N_GROUPS

BF16 = jnp.bfloat16
F32 = jnp.float32


def _cparams(sem):
    return pltpu.CompilerParams(dimension_semantics=sem, vmem_limit_bytes=VMEM_LIMIT)


def _layernorm(x):
    mu = jnp.mean(x, axis=-1, keepdims=True)
    xc = x - mu
    var = jnp.mean(xc * xc, axis=-1, keepdims=True)
    return xc * lax.rsqrt(var + LN_EPS)


def _split_hi_lo(a):
    hi = a.astype(BF16)
    lo = (a - hi.astype(F32)).astype(BF16)
    return hi, lo


def _dot(a, b):
    return jnp.dot(a, b, preferred_element_type=F32)


HALF = D_MODEL // 2
U32 = jnp.uint32
PLANES = HALF // LANES


def _pack_rows(x):
    lo = lax.bitcast_convert_type(x[:, :HALF].astype(BF16).astype(F32), U32)
    hi = lax.bitcast_convert_type(x[:, HALF:].astype(BF16).astype(F32), U32)
    return (lo >> 16) | hi


def _unpack_rows(w):
    lo = lax.bitcast_convert_type(w << 16, F32)
    hi = lax.bitcast_convert_type(w & jnp.uint32(0xFFFF0000), F32)
    return lo, hi


def _ada_kernel(c_ref, whi_ref, wlo_ref, b_ref, o_ref):
    c = c_ref[...]
    s = c * jax.nn.sigmoid(c)
    s_hi, s_lo = _split_hi_lo(s)
    acc = _dot(s_hi, whi_ref[...]) + _dot(s_lo, whi_ref[...]) + _dot(s_hi, wlo_ref[...])
    o_ref[...] = acc + b_ref[...]


def _ada(c, w_hi, w_lo, b):
    bsz = c.shape[0]
    n = w_hi.shape[1]
    tn = 1024
    return pl.pallas_call(
        _ada_kernel,
        out_shape=jax.ShapeDtypeStruct((bsz, n), F32),
        grid=(n // tn,),
        in_specs=[
            pl.BlockSpec((bsz, D_MODEL), lambda j: (0, 0)),
            pl.BlockSpec((D_MODEL, tn), lambda j: (0, j)),
            pl.BlockSpec((D_MODEL, tn), lambda j: (0, j)),
            pl.BlockSpec((1, tn), lambda j: (0, j)),
        ],
        out_specs=pl.BlockSpec((bsz, tn), lambda j: (0, j)),
        compiler_params=_cparams(("arbitrary",)),
        name="ada",
    )(c, w_hi, w_lo, b)


def _rope(x, cos_t, sin_t):
    lane = lax.broadcasted_iota(jnp.int32, x.shape, 1)
    upper = (lane & 32) != 0
    partner = jnp.where(upper, pltpu.roll(x, 32, axis=1), pltpu.roll(x, LANES - 32, axis=1))
    return x * cos_t + partner * sin_t


def _in_proj_kernel(x_ref, ada_ref, w_ref, cos_ref, sin_ref,
                    q_ref, k_ref, v_ref, cb_ref, u_ref):
    x = x_ref[...]
    ada = ada_ref[0]
    sh_m = ada[0:1, :]
    sc_m = ada[1:2, :]
    h = (_layernorm(x) * (1.0 + sc_m) + sh_m).astype(BF16)
    cos_t = cos_ref[...]
    sin_t = sin_ref[...]
    qk_scale = QK_HEAD_DIM ** -0.5 * math.log2(math.e)
    q = _dot(h, w_ref[:, 0:512])
    k = _dot(h, w_ref[:, 512:1024])
    for j in range(4):
        lo = j * LANES
        q_ref[:, lo:lo + LANES] = (_rope(q[:, lo:lo + LANES], cos_t, sin_t) * qk_scale).astype(BF16)
        k_ref[:, lo:lo + LANES] = _rope(k[:, lo:lo + LANES], cos_t, sin_t).astype(BF16)
    v_ref[...] = _dot(h, w_ref[:, 1024:1536]).astype(BF16)
    cb_ref[...] = _dot(h, w_ref[:, 1536:2048]).astype(BF16)
    cc = _dot(h, w_ref[:, 2048:2560])
    ch = _dot(h, w_ref[:, 2560:3072])
    u_ref[...] = (cc * ch).astype(BF16)


def _in_proj(x2d, ada3, w_in_bf, cos_t, sin_t, seq):
    t = x2d.shape[0]
    tm = TM_PROJ
    tiles_per_seq = seq // tm
    out = jax.ShapeDtypeStruct((t, 512), BF16)
    ospec = pl.BlockSpec((tm, 512), lambda i: (i, 0))
    return pl.pallas_call(
        _in_proj_kernel,
        out_shape=(out,) * 5,
        grid=(t // tm,),
        in_specs=[
            pl.BlockSpec((tm, D_MODEL), lambda i: (i, 0)),
            pl.BlockSpec((1, 6, D_MODEL), lambda i: (i // tiles_per_seq, 0, 0)),
            pl.BlockSpec((D_MODEL, IN_WIDTH), lambda i: (0, 0)),
            pl.BlockSpec((tm, LANES), lambda i: (i % tiles_per_seq, 0)),
            pl.BlockSpec((tm, LANES), lambda i: (i % tiles_per_seq, 0)),
        ],
        out_specs=(ospec,) * 5,
        compiler_params=_cparams(("arbitrary",)),
        name="in_proj",
    )(x2d, ada3, w_in_bf, cos_t, sin_t)


def _diff_attn_kernel(q_ref, k_ref, v_ref, lq1_ref, lk1_ref, lq2_ref, lk2_ref, g_ref, o_ref,
                      qq_ref, vx_ref, s_buf, m_ref, acc_ref, *, seq):
    nq = seq // TQ
    nk = seq // TK
    n_lane_blocks = TK // LANES

    vx_ref[:, 0:V_HEAD_DIM] = v_ref[0]
    vx_ref[:, V_HEAD_DIM:2 * V_HEAD_DIM] = jnp.ones((seq, V_HEAD_DIM), BF16)
    lam = (jnp.exp(jnp.sum(lq1_ref[...] * lk1_ref[...], axis=-1, keepdims=True))
           - jnp.exp(jnp.sum(lq2_ref[...] * lk2_ref[...], axis=-1, keepdims=True)) + LAMBDA_INIT)

    def load_q(qt, qslot):
        q = q_ref[0, pl.ds(pl.multiple_of(qt * TQ, TQ), TQ), :]
        lane = lax.broadcasted_iota(jnp.int32, q.shape, 1)
        zero = jnp.zeros_like(q)
        qq_ref[qslot, 0:TQ, :] = jnp.where(lane < QK_HEAD_DIM, q, zero)
        qq_ref[qslot, TQ:2 * TQ, :] = jnp.where(lane >= QK_HEAD_DIM, q, zero)

    def scores(qslot, j, slot):
        start = pl.multiple_of(j * TK, TK)
        kc = k_ref[0, pl.ds(start, TK), :]
        s_buf[slot] = lax.dot_general(qq_ref[qslot], kc, (((1,), (1,)), ((), ())), preferred_element_type=F32)

    def softmax_pv(j, slot):
        blocks = [s_buf[slot, :, c * LANES:(c + 1) * LANES] for c in range(n_lane_blocks)]
        mb = blocks[0]
        for c in range(1, n_lane_blocks):
            mb = jnp.maximum(mb, blocks[c])
        m_old = m_ref[...]
        m_new = jnp.maximum(m_old, jnp.max(mb, axis=-1, keepdims=True))
        alpha = jnp.exp2(m_old - m_new)
        m_ref[...] = m_new
        p = jnp.concatenate([jnp.exp2(blk - m_new).astype(BF16) for blk in blocks], axis=1)
        start = pl.multiple_of(j * TK, TK)
        pv = _dot(p, vx_ref[pl.ds(start, TK), :])
        acc_ref[:, 0:V_HEAD_DIM] = alpha * acc_ref[:, 0:V_HEAD_DIM] + pv[:, 0:V_HEAD_DIM]
        acc_ref[:, V_HEAD_DIM:] = alpha * acc_ref[:, V_HEAD_DIM:] + pv[:, V_HEAD_DIM:]

    def q_tile(qt, qslot, has_next):
        m_ref[...] = jnp.full(m_ref.shape, -jnp.inf, F32)
        acc_ref[...] = jnp.zeros(acc_ref.shape, F32)

        def pair(jj, c):
            j = 2 * jj
            scores(qslot, j + 1, 1)
            softmax_pv(j, 0)
            scores(qslot, j + 2, 0)
            softmax_pv(j + 1, 1)
            return c

        lax.fori_loop(0, nk // 2 - 1, pair, 0)
        scores(qslot, nk - 1, 1)
        softmax_pv(nk - 2, 0)
        if has_next:
            load_q(qt + 1, 1 - qslot)
            scores(1 - qslot, 0, 0)
        softmax_pv(nk - 1, 1)

        o = acc_ref[:, 0:V_HEAD_DIM] / acc_ref[:, V_HEAD_DIM:]
        of = o[:TQ] - lam * o[TQ:]
        of = of * lax.rsqrt(jnp.mean(of * of, axis=-1, keepdims=True) + RMS_EPS)
        of = of * g_ref[...] * (1.0 - LAMBDA_INIT)
        o_ref[0, pl.ds(pl.multiple_of(qt * TQ, TQ), TQ), :] = of.astype(BF16)

    load_q(0, 0)
    scores(0, 0, 0)

    def q_pair(i, c):
        q_tile(2 * i, 0, True)
        q_tile(2 * i + 1, 1, True)
        return c

    lax.fori_loop(0, nq // 2 - 1, q_pair, 0)
    q_tile(nq - 2, 0, True)
    q_tile(nq - 1, 1, False)


def _diff_attn(q, k, v, lq1, lk1, lq2, lk2, g):
    b, seq, _ = q.shape
    lam_spec = pl.BlockSpec((1, QK_HEAD_DIM), lambda bi, h: (0, 0))
    head_spec = pl.BlockSpec((1, seq, LANES), lambda bi, h: (bi, 0, h))
    return pl.pallas_call(
        functools.partial(_diff_attn_kernel, seq=seq),
        out_shape=jax.ShapeDtypeStruct((b, seq, ATTN_WIDTH), BF16),
        grid=(b, N_DIFF_HEADS),
        in_specs=[
            head_spec, head_spec, head_spec,
            lam_spec, lam_spec, lam_spec, lam_spec,
            pl.BlockSpec((1, V_HEAD_DIM), lambda bi, h: (0, 0)),
        ],
        out_specs=head_spec,
        scratch_shapes=[
            pltpu.VMEM((2, 2 * TQ, LANES), BF16),
            pltpu.VMEM((seq, 2 * V_HEAD_DIM), BF16),
            pltpu.VMEM((2, 2 * TQ, TK), F32),
            pltpu.VMEM((2 * TQ, LANES), F32),
            pltpu.VMEM((2 * TQ, 2 * V_HEAD_DIM), F32),
        ],
        compiler_params=_cparams(("arbitrary", "arbitrary")),
        name="diff_attn",
    )(q, k, v, lq1, lk1, lq2, lk2, g)


def _lane_min_index(mask, lane_f):
    return jnp.min(jnp.where(mask, lane_f, float(ROUTER_LANES)), axis=-1, keepdims=True)


def _mix_route_kernel(x_ref, attn_ref, cb_ref, u_ref, uprev_ref, unext_ref, ada_ref,
                      cw_ref, cbias_ref, woa_ref, woc_ref, g1_ref, b1_ref, wr_ref, br_ref,
                      x1_ref, h2_ref, rt_ref, cnt_ref, base_ref, *, seq):
    i = pl.program_id(0)
    tm = TM_MIX

    @pl.when(i == 0)
    def _():
        base_ref[...] = jnp.zeros_like(base_ref)

    ada = ada_ref[0]
    g_m = ada[2:3, :]
    sh_f = ada[3:4, :]
    sc_f = ada[4:5, :]

    u = u_ref[...].astype(F32)
    row = lax.broadcasted_iota(jnp.int32, u.shape, 0)
    not_seq_start = ((i * tm) % seq != 0).astype(F32)
    not_seq_end = (((i + 1) * tm) % seq != 0).astype(F32)
    halo_prev = uprev_ref[...].astype(F32)[BF16_SUBLANES - 1:BF16_SUBLANES, :] * not_seq_start
    halo_next = unext_ref[...].astype(F32)[0:1, :] * not_seq_end
    u_prev = jnp.where(row == 0, halo_prev, pltpu.roll(u, 1, axis=0))
    u_next = jnp.where(row == tm - 1, halo_next, pltpu.roll(u, tm - 1, axis=0))
    cw = cw_ref[...]
    y = cbias_ref[...] + u_prev * cw[0:1, :]
    y = y + u * cw[1:2, :]
    y = y + u_next * cw[2:3, :]
    conv = (cb_ref[...].astype(F32) * y).astype(BF16)

    mix = _dot(attn_ref[...], woa_ref[...]) + _dot(conv, woc_ref[...])
    x1 = _layernorm(DEEPNORM_ALPHA * x_ref[...] + g_m * mix) * g1_ref[...] + b1_ref[...]
    x1_ref[...] = x1
    h2 = _layernorm(x1) * (1.0 + sc_f) + sh_f
    for c, plane in enumerate(_to_planes(_pack_rows(h2))):
        h2_ref[c] = plane

    h_hi, h_lo = _split_hi_lo(h2)
    both = _dot(h_hi, wr_ref[...])
    logits = (both[:, :ROUTER_LANES] + both[:, ROUTER_LANES:]
              + _dot(h_lo, wr_ref[:, :ROUTER_LANES]) + br_ref[...])

    lane = lax.broadcasted_iota(jnp.int32, logits.shape, 1)
    lane_f = lane.astype(F32)
    neg = -jnp.inf
    is_group = lane < N_GROUPS
    lg = jnp.where(is_group, logits, neg)
    lg_max = jnp.max(lg, axis=-1, keepdims=True)
    g_sel = _lane_min_index(lg == lg_max, lane_f)
    pg_sel = 1.0 / jnp.sum(jnp.exp(lg - lg_max), axis=-1, keepdims=True)

    first = EXPERT_LANE0 + EXPERTS_PER_GROUP * g_sel
    in_group = (lane_f >= first) & (lane_f < first + EXPERTS_PER_GROUP)
    le = jnp.where(in_group, logits, neg)
    l0 = jnp.max(le, axis=-1, keepdims=True)
    i0 = _lane_min_index(le == l0, lane_f)
    le2 = jnp.where(lane_f == i0, neg, le)
    l1 = jnp.max(le2, axis=-1, keepdims=True)
    i1 = _lane_min_index(le2 == l1, lane_f)
    t_exp = jnp.exp(l1 - l0)
    p0 = 1.0 / (1.0 + t_exp)
    w0 = pg_sel * p0
    w1 = pg_sel * (t_exp * p0)

    oh0 = lane_f == i0
    oh1 = lane_f == i1
    onehots = jnp.concatenate([jnp.where(oh0, 1.0, 0.0), jnp.where(oh1, 1.0, 0.0)], axis=1).astype(BF16)
    r_i = lax.broadcasted_iota(jnp.int32, (tm, tm), 0)
    c_i = lax.broadcasted_iota(jnp.int32, (tm, tm), 1)
    tri = jnp.where(c_i < r_i, 1.0, 0.0).astype(BF16)
    before = _dot(tri, onehots)
    cnt0 = jnp.sum(jnp.where(oh0, 1.0, 0.0), axis=0, keepdims=True)
    cnt1 = jnp.sum(jnp.where(oh1, 1.0, 0.0), axis=0, keepdims=True)
    base = base_ref[...]
    rank0 = jnp.sum(jnp.where(oh0, before[:, :ROUTER_LANES] + base, 0.0), axis=-1, keepdims=True)
    rank1 = jnp.sum(jnp.where(oh1, before[:, ROUTER_LANES:] + base + cnt0, 0.0), axis=-1, keepdims=True)
    new_base = base + cnt0 + cnt1
    base_ref[...] = new_base
    cnt_ref[...] = new_base

    rt = jnp.where(lane == 0, i0 - EXPERT_LANE0, 0.0)
    rt = jnp.where(lane == 1, i1 - EXPERT_LANE0, rt)
    rt = jnp.where(lane == 2, rank0, rt)
    rt = jnp.where(lane == 3, rank1, rt)
    rt = jnp.where(lane == 4, w0, rt)
    rt = jnp.where(lane == 5, w1, rt)
    rt_ref[...] = rt


def _mix_route(x2d, attn2d, cb, u, ada3, conv_w, conv_b, woa, woc, g1, b1, wr, br, seq):
    t = x2d.shape[0]
    tm = TM_MIX
    tiles_per_seq = seq // tm
    hb = tm // BF16_SUBLANES
    n_halo = t // BF16_SUBLANES
    const = lambda i: (0, 0)
    return pl.pallas_call(
        functools.partial(_mix_route_kernel, seq=seq),
        out_shape=(
            jax.ShapeDtypeStruct((t, D_MODEL), F32),
            jax.ShapeDtypeStruct((PLANES, t, LANES), U32),
            jax.ShapeDtypeStruct((t, ROUTER_LANES), F32),
            jax.ShapeDtypeStruct((1, ROUTER_LANES), F32),
        ),
        grid=(t // tm,),
        in_specs=[
            pl.BlockSpec((tm, D_MODEL), lambda i: (i, 0)),
            pl.BlockSpec((tm, ATTN_WIDTH), lambda i: (i, 0)),
            pl.BlockSpec((tm, CONV_WIDTH), lambda i: (i, 0)),
            pl.BlockSpec((tm, CONV_WIDTH), lambda i: (i, 0)),
            pl.BlockSpec((BF16_SUBLANES, CONV_WIDTH), lambda i: (jnp.maximum(i * hb - 1, 0), 0)),
            pl.BlockSpec((BF16_SUBLANES, CONV_WIDTH), lambda i: (jnp.minimum((i + 1) * hb, n_halo - 1), 0)),
            pl.BlockSpec((1, 6, D_MODEL), lambda i: (i // tiles_per_seq, 0, 0)),
            pl.BlockSpec((CONV_K, CONV_WIDTH), const),
            pl.BlockSpec((1, CONV_WIDTH), const),
            pl.BlockSpec((ATTN_WIDTH, D_MODEL), const),
            pl.BlockSpec((CONV_WIDTH, D_MODEL), const),
            pl.BlockSpec((1, D_MODEL), const),
            pl.BlockSpec((1, D_MODEL), const),
            pl.BlockSpec((D_MODEL, 2 * ROUTER_LANES), const),
            pl.BlockSpec((1, ROUTER_LANES), const),
        ],
        out_specs=(
            pl.BlockSpec((tm, D_MODEL), lambda i: (i, 0)),
            pl.BlockSpec((PLANES, tm, LANES), lambda i: (0, i, 0)),
            pl.BlockSpec((tm, ROUTER_LANES), lambda i: (i, 0)),
            pl.BlockSpec((1, ROUTER_LANES), const),
        ),
        scratch_shapes=[pltpu.VMEM((1, ROUTER_LANES), F32)],
        compiler_params=_cparams(("arbitrary",)),
        name="mix_route",
    )(x2d, attn2d, cb, u, u, u, ada3, conv_w, conv_b, woa, woc, g1, b1, wr, br)


def _sc_mesh():
    return plsc.VectorSubcoreMesh(core_axis_name="core", subcore_axis_name="subcore")


def _sc_scatter_rows(src, idx, n_dst):
    m = idx.shape[0]
    w = SC_WINDOW
    n_src_blocks = src.shape[0] // w

    @pl.kernel(out_type=jax.ShapeDtypeStruct((n_dst, LANES), src.dtype), mesh=_sc_mesh(), scratch_types=[])
    def scatter(x_hbm, i_hbm, o_hbm):
        def body(x_vmem, i_vmem):
            pltpu.sync_copy(x_vmem, o_hbm.at[i_vmem.at[0]])

        pltpu.emit_pipeline(
            body, grid=(m // w,),
            in_specs=[pl.BlockSpec((w, LANES), index_map=lambda i: (i % n_src_blocks, 0)),
                      pl.BlockSpec((1, w), index_map=lambda i: (0, i))],
            out_specs=[],
            core_axis_name=("core", "subcore"),
            dimension_semantics=(pltpu.PARALLEL,),
        )(x_hbm, i_hbm)

    return scatter(src, idx.reshape(1, m))


def _sc_gather_rows(src, idx):
    m = idx.shape[0]
    w = SC_WINDOW

    @pl.kernel(out_type=jax.ShapeDtypeStruct((m, LANES), src.dtype), mesh=_sc_mesh(), scratch_types=[])
    def gather(x_hbm, i_hbm, o_hbm):
        def body(i_vmem, o_vmem):
            pltpu.sync_copy(x_hbm.at[i_vmem.at[0]], o_vmem)

        pltpu.emit_pipeline(
            body, grid=(m // w,),
            in_specs=[pl.BlockSpec((1, w), index_map=lambda i: (0, i))],
            out_specs=[pl.BlockSpec((w, LANES), index_map=lambda i: (i, 0))],
            core_axis_name=("core", "subcore"),
            dimension_semantics=(pltpu.PARALLEL,),
        )(i_hbm, o_hbm)

    return gather(src, idx.reshape(1, m))


def _to_planes(words):
    return [words[:, c * LANES:(c + 1) * LANES] for c in range(PLANES)]


def _experts_kernel(blk_exp_ref, n_used_ref, xs_ref, wg_ref, wu_ref, wd_ref, y_ref):
    del blk_exp_ref
    used = pl.program_id(0) < n_used_ref[0]

    @pl.when(jnp.logical_not(used))
    def _():
        y_ref[...] = jnp.zeros_like(y_ref)

    @pl.when(used)
    def _():
        x_lo, x_hi = _unpack_rows(jnp.concatenate([xs_ref[c] for c in range(PLANES)], axis=1))
        x_lo = x_lo.astype(BF16)
        x_hi = x_hi.astype(BF16)
        g = _dot(x_lo, wg_ref[0, :HALF, :]) + _dot(x_hi, wg_ref[0, HALF:, :])
        up = _dot(x_lo, wu_ref[0, :HALF, :]) + _dot(x_hi, wu_ref[0, HALF:, :])
        act = (g * jax.nn.sigmoid(g) * up).astype(BF16)
        for c, plane in enumerate(_to_planes(_pack_rows(_dot(act, wd_ref[0])))):
            y_ref[c] = plane


def _experts(blk_exp, n_used, xs, wg, wu, wd):
    rows = xs.shape[1]
    blk = EXPERT_BLOCK
    return pl.pallas_call(
        _experts_kernel,
        out_shape=jax.ShapeDtypeStruct((PLANES, rows, LANES), U32),
        grid_spec=pltpu.PrefetchScalarGridSpec(
            num_scalar_prefetch=2,
            grid=(rows // blk,),
            in_specs=[
                pl.BlockSpec((PLANES, blk, LANES), lambda i, be, nu: (0, i, 0)),
                pl.BlockSpec((1, D_MODEL, D_FF_EXPERT), lambda i, be, nu: (be[i], 0, 0)),
                pl.BlockSpec((1, D_MODEL, D_FF_EXPERT), lambda i, be, nu: (be[i], 0, 0)),
                pl.BlockSpec((1, D_FF_EXPERT, D_MODEL), lambda i, be, nu: (be[i], 0, 0)),
            ],
            out_specs=pl.BlockSpec((PLANES, blk, LANES), lambda i, be, nu: (0, i, 0)),
        ),
        compiler_params=_cparams(("arbitrary",)),
        name="experts",
    )(blk_exp, n_used, xs, wg, wu, wd)


def _combine_kernel(x1_ref, rt_ref, ada_ref, g2_ref, b2_ref, yg_ref, o_ref):
    g_f = ada_ref[0][5:6, :]
    rt = rt_ref[...]
    w0 = rt[:, 4:5]
    w1 = rt[:, 5:6]
    y0_lo, y0_hi = _unpack_rows(jnp.concatenate([yg_ref[0, c] for c in range(PLANES)], axis=1))
    y1_lo, y1_hi = _unpack_rows(jnp.concatenate([yg_ref[1, c] for c in range(PLANES)], axis=1))
    f = jnp.concatenate([y0_lo * w0 + y1_lo * w1, y0_hi * w0 + y1_hi * w1], axis=1)
    z = DEEPNORM_ALPHA * x1_ref[...] + g_f * f
    o_ref[...] = _layernorm(z) * g2_ref[...] + b2_ref[...]


def _combine(x1, rt, ada3, g2, b2, yg, seq):
    t = x1.shape[0]
    tm = TM_COMBINE
    tiles_per_seq = seq // tm
    const = lambda i: (0, 0)
    return pl.pallas_call(
        _combine_kernel,
        out_shape=jax.ShapeDtypeStruct((t, D_MODEL), F32),
        grid=(t // tm,),
        in_specs=[
            pl.BlockSpec((tm, D_MODEL), lambda i: (i, 0)),
            pl.BlockSpec((tm, ROUTER_LANES), lambda i: (i, 0)),
            pl.BlockSpec((1, 6, D_MODEL), lambda i: (i // tiles_per_seq, 0, 0)),
            pl.BlockSpec((1, D_MODEL), const),
            pl.BlockSpec((1, D_MODEL), const),
            pl.BlockSpec((2, PLANES, tm, LANES), lambda i: (0, 0, i, 0)),
        ],
        out_specs=pl.BlockSpec((tm, D_MODEL), lambda i: (i, 0)),
        compiler_params=_cparams(("arbitrary",)),
        name="combine",
    )(x1, rt, ada3, g2, b2, yg)


def _rope_tables(seq):
    half = QK_HEAD_DIM // 2
    inv = 1.0 / (ROPE_THETA ** (jnp.arange(0, QK_HEAD_DIM, 2, dtype=F32) / QK_HEAD_DIM))
    ang = jnp.arange(seq, dtype=F32)[:, None] * inv[None, :]
    cos, sin = jnp.cos(ang), jnp.sin(ang)
    cos_t = jnp.tile(cos, (1, LANES // half))
    sin_t = jnp.tile(jnp.concatenate([-sin, sin], axis=1), (1, LANES // QK_HEAD_DIM))
    return cos_t, sin_t


def _routing_tables(rt, counts, n_rows_max):
    t = rt.shape[0]
    blk = EXPERT_BLOCK
    counts = counts[0, EXPERT_LANE0:EXPERT_LANE0 + N_EXPERTS].astype(jnp.int32)
    padded = ((counts + blk - 1) // blk) * blk
    pend = jnp.cumsum(padded)
    pstart = pend - padded
    e = rt[:, 0:2].astype(jnp.int32)
    rank = rt[:, 2:4].astype(jnp.int32)
    dest = pstart[e] + rank
    plane0 = (jnp.arange(PLANES, dtype=jnp.int32) * n_rows_max)[None, :, None]
    row_idx = (dest.T[:, None, :] + plane0).reshape(2 * PLANES * t)
    blk_start = jnp.arange(n_rows_max // blk, dtype=jnp.int32) * blk
    blk_exp = jnp.minimum(jnp.sum(blk_start[:, None] >= pend[None, :], axis=1), N_EXPERTS - 1).astype(jnp.int32)
    n_used = (pend[-1:] // blk).astype(jnp.int32)
    return row_idx, blk_exp, n_used


def _encoder(x, c, p):
    b, seq, d = x.shape
    t = b * seq
    x2d = x.reshape(t, d)
    ada3 = _ada(c, p["w_ada_hi"], p["w_ada_lo"], p["b_ada"]).reshape(b, 6, d)
    cos_t, sin_t = _rope_tables(seq)
    q, k, v, cb, u = _in_proj(x2d, ada3, p["w_in"], cos_t, sin_t, seq)
    attn = _diff_attn(q.reshape(b, seq, 512), k.reshape(b, seq, 512), v.reshape(b, seq, 512),
                      p["lq1"], p["lk1"], p["lq2"], p["lk2"], p["subln_g"])
    x1, h2, rt, counts = _mix_route(x2d, attn.reshape(t, 512), cb, u, ada3, p["conv_w"], p["conv_b"],
                                    p["woa"], p["woc"], p["ln1_g"], p["ln1_b"], p["wr"], p["br"], seq)
    n_rows_max = (2 * t // EXPERT_BLOCK + N_EXPERTS) * EXPERT_BLOCK
    row_idx, blk_exp, n_used = _routing_tables(rt, counts, n_rows_max)
    xs = _sc_scatter_rows(h2.reshape(PLANES * t, LANES), row_idx, PLANES * n_rows_max)
    y = _experts(blk_exp, n_used, xs.reshape(PLANES, n_rows_max, LANES), p["wg"], p["wu"], p["wd"])
    yg = _sc_gather_rows(y.reshape(PLANES * n_rows_max, LANES), row_idx)
    out = _combine(x1, rt, ada3, p["ln2_g"], p["ln2_b"], yg.reshape(2, PLANES, t, LANES), seq)
    return out.reshape(b, seq, d)


def kernel(x_prompt, x_sample, c_prompt, c_sample, w_ada, b_ada, w_in, lambda_q1, lambda_k1, lambda_q2, lambda_k2, attn_subln_g, conv_w, conv_b, w_out, ln1_g, ln1_b, router_group_w, router_group_b, router_expert_w, router_expert_b, expert_w_gate, expert_w_up, expert_w_down, ln2_g, ln2_b):
    l = 0
    w_ada_hi, w_ada_lo = _split_hi_lo(w_ada[l])
    wr = jnp.concatenate([router_group_w[l], router_expert_w[l]], axis=1)
    wr = jnp.pad(wr, ((0, 0), (0, ROUTER_LANES - wr.shape[1])))
    wr_hi, wr_lo = _split_hi_lo(wr)
    br = jnp.concatenate([router_group_b[l], router_expert_b[l]])
    br = jnp.pad(br, (0, ROUTER_LANES - br.shape[0])).reshape(1, ROUTER_LANES)
    w_out_bf = w_out[l].astype(BF16)
    p = {
        "w_ada_hi": w_ada_hi, "w_ada_lo": w_ada_lo, "b_ada": b_ada[l].reshape(1, -1),
        "w_in": w_in[l].astype(BF16),
        "lq1": lambda_q1[l].reshape(1, -1), "lk1": lambda_k1[l].reshape(1, -1),
        "lq2": lambda_q2[l].reshape(1, -1), "lk2": lambda_k2[l].reshape(1, -1),
        "subln_g": attn_subln_g[l].reshape(1, -1),
        "conv_w": conv_w[l], "conv_b": conv_b[l].reshape(1, -1),
        "woa": w_out_bf[:ATTN_WIDTH], "woc": w_out_bf[ATTN_WIDTH:],
        "ln1_g": ln1_g[l].reshape(1, -1), "ln1_b": ln1_b[l].reshape(1, -1),
        "wr": jnp.concatenate([wr_hi, wr_lo], axis=1), "br": br,
        "wg": expert_w_gate[l].astype(BF16), "wu": expert_w_up[l].astype(BF16),
        "wd": expert_w_down[l].astype(BF16),
        "ln2_g": ln2_g[l].reshape(1, -1), "ln2_b": ln2_b[l].reshape(1, -1),
    }
    y_prompt = _encoder(x_prompt, c_prompt, p)
    y_sample = _encoder(x_sample, c_sample, p)
    return (y_prompt, y_sample)
```

```python
import functools
import math

import jax
import jax.numpy as jnp
from jax import lax
from jax.experimental import pallas as pl
from jax.experimental.pallas import tpu as pltpu
from jax.experimental.pallas import tpu_sc as plsc

D_MODEL = 1024
ATTN_WIDTH = 512
CONV_WIDTH = 512
QK_HEAD_DIM = 64
V_HEAD_DIM = 128
N_DIFF_HEADS = 4
IN_WIDTH = 3072
CONV_K = 3
ROPE_THETA = 10000.0
N_GROUPS = 4
EXPERTS_PER_GROUP = 8
N_EXPERTS = 32
D_FF_EXPERT = 512
LN_EPS = 1e-5
RMS_EPS = 1e-5
DEPTH = 1
DEEPNORM_ALPHA = (2.0 * DEPTH) ** 0.25
LAMBDA_INIT = 0.8 - 0.6 * math.exp(-0.3 * 0)

LANES = 128
BF16_SUBLANES = 16
VMEM_LIMIT = 48 * 1024 * 1024

TM_PROJ = 512
TQ = 512
TK = 512
TM_MIX = 1024
TM_COMBINE = 512
SC_WINDOW = 256
EXPERT_BLOCK = 512
ROUTER_LANES = 128
EXPERT_LANE0 = N_GROUPS
ROUTE_FIELDS = 8

BF16 = jnp.bfloat16
F32 = jnp.float32


def _cparams(sem):
    return pltpu.CompilerParams(dimension_semantics=sem, vmem_limit_bytes=VMEM_LIMIT)


def _layernorm(x):
    mu = jnp.mean(x, axis=-1, keepdims=True)
    xc = x - mu
    var = jnp.mean(xc * xc, axis=-1, keepdims=True)
    return xc * lax.rsqrt(var + LN_EPS)


def _split_hi_lo(a):
    hi = a.astype(BF16)
    lo = (a - hi.astype(F32)).astype(BF16)
    return hi, lo


def _dot(a, b):
    return jnp.dot(a, b, preferred_element_type=F32)


HALF = D_MODEL // 2
U32 = jnp.uint32
PLANES = HALF // LANES


def _pack_rows(x):
    lo = lax.bitcast_convert_type(x[:, :HALF].astype(BF16).astype(F32), U32)
    hi = lax.bitcast_convert_type(x[:, HALF:].astype(BF16).astype(F32), U32)
    return (lo >> 16) | hi


def _unpack_rows(w):
    lo = lax.bitcast_convert_type(w << 16, F32)
    hi = lax.bitcast_convert_type(w & jnp.uint32(0xFFFF0000), F32)
    return lo, hi


def _ada_kernel(c_ref, whi_ref, wlo_ref, b_ref, o_ref):
    c = c_ref[...]
    s = c * jax.nn.sigmoid(c)
    s_hi, s_lo = _split_hi_lo(s)
    acc = _dot(s_hi, whi_ref[...]) + _dot(s_lo, whi_ref[...]) + _dot(s_hi, wlo_ref[...])
    o_ref[...] = acc + b_ref[...]


def _ada(c, w_hi, w_lo, b):
    bsz = c.shape[0]
    n = w_hi.shape[1]
    tn = 1024
    return pl.pallas_call(
        _ada_kernel,
        out_shape=jax.ShapeDtypeStruct((bsz, n), F32),
        grid=(n // tn,),
        in_specs=[
            pl.BlockSpec((bsz, D_MODEL), lambda j: (0, 0)),
            pl.BlockSpec((D_MODEL, tn), lambda j: (0, j)),
            pl.BlockSpec((D_MODEL, tn), lambda j: (0, j)),
            pl.BlockSpec((1, tn), lambda j: (0, j)),
        ],
        out_specs=pl.BlockSpec((bsz, tn), lambda j: (0, j)),
        compiler_params=_cparams(("arbitrary",)),
        name="ada",
    )(c, w_hi, w_lo, b)


def _rope(x, cos_t, sin_t):
    lane = lax.broadcasted_iota(jnp.int32, x.shape, 1)
    upper = (lane & 32) != 0
    partner = jnp.where(upper, pltpu.roll(x, 32, axis=1), pltpu.roll(x, LANES - 32, axis=1))
    return x * cos_t + partner * sin_t


def _in_proj_kernel(x_ref, ada_ref, w_ref, cos_ref, sin_ref,
                    q_ref, k_ref, v_ref, cb_ref, u_ref):
    x = x_ref[...]
    ada = ada_ref[0]
    sh_m = ada[0:1, :]
    sc_m = ada[1:2, :]
    h = (_layernorm(x) * (1.0 + sc_m) + sh_m).astype(BF16)
    cos_t = cos_ref[...]
    sin_t = sin_ref[...]
    qk_scale = QK_HEAD_DIM ** -0.5 * math.log2(math.e)
    q = _dot(h, w_ref[:, 0:512])
    k = _dot(h, w_ref[:, 512:1024])
    for j in range(4):
        lo = j * LANES
        q_ref[:, lo:lo + LANES] = (_rope(q[:, lo:lo + LANES], cos_t, sin_t) * qk_scale).astype(BF16)
        k_ref[:, lo:lo + LANES] = _rope(k[:, lo:lo + LANES], cos_t, sin_t).astype(BF16)
    v_ref[...] = _dot(h, w_ref[:, 1024:1536]).astype(BF16)
    cb_ref[...] = _dot(h, w_ref[:, 1536:2048]).astype(BF16)
    cc = _dot(h, w_ref[:, 2048:2560])
    ch = _dot(h, w_ref[:, 2560:3072])
    u_ref[...] = (cc * ch).astype(BF16)


def _in_proj(x2d, ada3, w_in_bf, cos_t, sin_t, seq):
    t = x2d.shape[0]
    tm = TM_PROJ
    tiles_per_seq = seq // tm
    out = jax.ShapeDtypeStruct((t, 512), BF16)
    ospec = pl.BlockSpec((tm, 512), lambda i: (i, 0))
    return pl.pallas_call(
        _in_proj_kernel,
        out_shape=(out,) * 5,
        grid=(t // tm,),
        in_specs=[
            pl.BlockSpec((tm, D_MODEL), lambda i: (i, 0)),
            pl.BlockSpec((1, 6, D_MODEL), lambda i: (i // tiles_per_seq, 0, 0)),
            pl.BlockSpec((D_MODEL, IN_WIDTH), lambda i: (0, 0)),
            pl.BlockSpec((tm, LANES), lambda i: (i % tiles_per_seq, 0)),
            pl.BlockSpec((tm, LANES), lambda i: (i % tiles_per_seq, 0)),
        ],
        out_specs=(ospec,) * 5,
        compiler_params=_cparams(("arbitrary",)),
        name="in_proj",
    )(x2d, ada3, w_in_bf, cos_t, sin_t)


def _diff_attn_kernel(q_ref, k_ref, v_ref, lq1_ref, lk1_ref, lq2_ref, lk2_ref, g_ref, o_ref,
                      qq_ref, vx_ref, s_buf, m_ref, acc_ref, *, seq, tk):
    nq = seq // TQ
    nk = seq // tk
    n_lane_blocks = tk // LANES

    vx_ref[:, 0:V_HEAD_DIM] = v_ref[0]
    vx_ref[:, V_HEAD_DIM:2 * V_HEAD_DIM] = jnp.ones((seq, V_HEAD_DIM), BF16)
    lam = (jnp.exp(jnp.sum(lq1_ref[...] * lk1_ref[...], axis=-1, keepdims=True))
           - jnp.exp(jnp.sum(lq2_ref[...] * lk2_ref[...], axis=-1, keepdims=True)) + LAMBDA_INIT)

    def load_q(qt, qslot):
        q = q_ref[0, pl.ds(pl.multiple_of(qt * TQ, TQ), TQ), :]
        lane = lax.broadcasted_iota(jnp.int32, q.shape, 1)
        zero = jnp.zeros_like(q)
        qq_ref[qslot, 0:TQ, :] = jnp.where(lane < QK_HEAD_DIM, q, zero)
        qq_ref[qslot, TQ:2 * TQ, :] = jnp.where(lane >= QK_HEAD_DIM, q, zero)

    def scores(qslot, j, slot):
        start = pl.multiple_of(j * tk, tk)
        kc = k_ref[0, pl.ds(start, tk), :]
        s_buf[slot] = lax.dot_general(qq_ref[qslot], kc, (((1,), (1,)), ((), ())), preferred_element_type=F32)

    def softmax_pv(j, slot):
        blocks = [s_buf[slot, :, c * LANES:(c + 1) * LANES] for c in range(n_lane_blocks)]
        mb = blocks[0]
        for c in range(1, n_lane_blocks):
            mb = jnp.maximum(mb, blocks[c])
        m_old = m_ref[...]
        m_new = jnp.maximum(m_old, jnp.max(mb, axis=-1, keepdims=True))
        alpha = jnp.exp2(m_old - m_new)
        m_ref[...] = m_new
        p = jnp.concatenate([jnp.exp2(blk - m_new).astype(BF16) for blk in blocks], axis=1)
        start = pl.multiple_of(j * tk, tk)
        pv = _dot(p, vx_ref[pl.ds(start, tk), :])
        acc_ref[:, 0:V_HEAD_DIM] = alpha * acc_ref[:, 0:V_HEAD_DIM] + pv[:, 0:V_HEAD_DIM]
        acc_ref[:, V_HEAD_DIM:] = alpha * acc_ref[:, V_HEAD_DIM:] + pv[:, V_HEAD_DIM:]

    def q_tile(qt, qslot, has_next):
        m_ref[...] = jnp.full(m_ref.shape, -jnp.inf, F32)
        acc_ref[...] = jnp.zeros(acc_ref.shape, F32)

        def pair(jj, c):
            j = 2 * jj
            scores(qslot, j + 1, 1)
            softmax_pv(j, 0)
            scores(qslot, j + 2, 0)
            softmax_pv(j + 1, 1)
            return c

        lax.fori_loop(0, nk // 2 - 1, pair, 0)
        scores(qslot, nk - 1, 1)
        softmax_pv(nk - 2, 0)
        if has_next:
            load_q(qt + 1, 1 - qslot)
            scores(1 - qslot, 0, 0)
        softmax_pv(nk - 1, 1)

        o = acc_ref[:, 0:V_HEAD_DIM] / acc_ref[:, V_HEAD_DIM:]
        of = o[:TQ] - lam * o[TQ:]
        of = of * lax.rsqrt(jnp.mean(of * of, axis=-1, keepdims=True) + RMS_EPS)
        of = of * g_ref[...] * (1.0 - LAMBDA_INIT)
        o_ref[0, pl.ds(pl.multiple_of(qt * TQ, TQ), TQ), :] = of.astype(BF16)

    load_q(0, 0)
    scores(0, 0, 0)

    def q_pair(i, c):
        q_tile(2 * i, 0, True)
        q_tile(2 * i + 1, 1, True)
        return c

    lax.fori_loop(0, nq // 2 - 1, q_pair, 0)
    q_tile(nq - 2, 0, True)
    q_tile(nq - 1, 1, False)


def _diff_attn(q, k, v, lq1, lk1, lq2, lk2, g):
    b, seq, _ = q.shape
    tk = TK
    lam_spec = pl.BlockSpec((1, QK_HEAD_DIM), lambda bi, h: (0, 0))
    head_spec = pl.BlockSpec((1, seq, LANES), lambda bi, h: (bi, 0, h))
    return pl.pallas_call(
        functools.partial(_diff_attn_kernel, seq=seq, tk=tk),
        out_shape=jax.ShapeDtypeStruct((b, seq, ATTN_WIDTH), BF16),
        grid=(b, N_DIFF_HEADS),
        in_specs=[
            head_spec, head_spec, head_spec,
            lam_spec, lam_spec, lam_spec, lam_spec,
            pl.BlockSpec((1, V_HEAD_DIM), lambda bi, h: (0, 0)),
        ],
        out_specs=head_spec,
        scratch_shapes=[
            pltpu.VMEM((2, 2 * TQ, LANES), BF16),
            pltpu.VMEM((seq, 2 * V_HEAD_DIM), BF16),
            pltpu.VMEM((2, 2 * TQ, tk), F32),
            pltpu.VMEM((2 * TQ, LANES), F32),
            pltpu.VMEM((2 * TQ, 2 * V_HEAD_DIM), F32),
        ],
        compiler_params=_cparams(("arbitrary", "arbitrary")),
        name="diff_attn",
    )(q, k, v, lq1, lk1, lq2, lk2, g)


def _lane_min_index(mask, lane_f):
    return jnp.min(jnp.where(mask, lane_f, float(ROUTER_LANES)), axis=-1, keepdims=True)


def _mix_route_kernel(x_ref, attn_ref, cb_ref, u_ref, uprev_ref, unext_ref, ada_ref,
                      cw_ref, cbias_ref, woa_ref, woc_ref, g1_ref, b1_ref, wr_ref, br_ref,
                      x1_ref, h2_ref, rt_ref, rtt_ref, cnt_ref, base_ref, *, seq):
    i = pl.program_id(0)
    tm = TM_MIX

    @pl.when(i == 0)
    def _():
        base_ref[...] = jnp.zeros_like(base_ref)

    ada = ada_ref[0]
    g_m = ada[2:3, :]
    sh_f = ada[3:4, :]
    sc_f = ada[4:5, :]

    u = u_ref[...].astype(F32)
    row = lax.broadcasted_iota(jnp.int32, u.shape, 0)
    not_seq_start = ((i * tm) % seq != 0).astype(F32)
    not_seq_end = (((i + 1) * tm) % seq != 0).astype(F32)
    halo_prev = uprev_ref[...].astype(F32)[BF16_SUBLANES - 1:BF16_SUBLANES, :] * not_seq_start
    halo_next = unext_ref[...].astype(F32)[0:1, :] * not_seq_end
    u_prev = jnp.where(row == 0, halo_prev, pltpu.roll(u, 1, axis=0))
    u_next = jnp.where(row == tm - 1, halo_next, pltpu.roll(u, tm - 1, axis=0))
    cw = cw_ref[...]

    hm = tm // 2
    halves = [slice(h * hm, (h + 1) * hm) for h in range(2)]
    mixes = []
    for rows in halves:
        y = cbias_ref[...] + u_prev[rows, :] * cw[0:1, :]
        y = y + u[rows, :] * cw[1:2, :]
        y = y + u_next[rows, :] * cw[2:3, :]
        conv = (cb_ref[rows, :].astype(F32) * y).astype(BF16)
        mixes.append(_dot(attn_ref[rows, :], woa_ref[...]) + _dot(conv, woc_ref[...]))

    lane = lax.broadcasted_iota(jnp.int32, (hm, ROUTER_LANES), 1)
    lane_f = lane.astype(F32)
    neg = -jnp.inf
    tri = jnp.where(lax.broadcasted_iota(jnp.int32, (hm, hm), 1) < lax.broadcasted_iota(jnp.int32, (hm, hm), 0),
                    1.0, 0.0).astype(BF16)
    base = base_ref[...]
    for rows, mix in zip(halves, mixes):
        x1 = _layernorm(DEEPNORM_ALPHA * x_ref[rows, :] + g_m * mix) * g1_ref[...] + b1_ref[...]
        x1_ref[rows, :] = x1
        h2 = _layernorm(x1) * (1.0 + sc_f) + sh_f
        for c, plane in enumerate(_to_planes(_pack_rows(h2))):
            h2_ref[c, rows, :] = plane

        h_hi, h_lo = _split_hi_lo(h2)
        both = _dot(h_hi, wr_ref[...])
        logits = (both[:, :ROUTER_LANES] + both[:, ROUTER_LANES:]
                  + _dot(h_lo, wr_ref[:, :ROUTER_LANES]) + br_ref[...])

        lg = jnp.where(lane < N_GROUPS, logits, neg)
        lg_max = jnp.max(lg, axis=-1, keepdims=True)
        g_sel = _lane_min_index(lg == lg_max, lane_f)
        pg_sel = 1.0 / jnp.sum(jnp.exp(lg - lg_max), axis=-1, keepdims=True)

        first = EXPERT_LANE0 + EXPERTS_PER_GROUP * g_sel
        in_group = (lane_f >= first) & (lane_f < first + EXPERTS_PER_GROUP)
        le = jnp.where(in_group, logits, neg)
        l0 = jnp.max(le, axis=-1, keepdims=True)
        i0 = _lane_min_index(le == l0, lane_f)
        le2 = jnp.where(lane_f == i0, neg, le)
        l1 = jnp.max(le2, axis=-1, keepdims=True)
        i1 = _lane_min_index(le2 == l1, lane_f)
        t_exp = jnp.exp(l1 - l0)
        p0 = 1.0 / (1.0 + t_exp)
        w0 = pg_sel * p0
        w1 = pg_sel * (t_exp * p0)

        oh0 = lane_f == i0
        oh1 = lane_f == i1
        onehots = jnp.concatenate([jnp.where(oh0, 1.0, 0.0), jnp.where(oh1, 1.0, 0.0)], axis=1).astype(BF16)
        before = _dot(tri, onehots)
        cnt0 = jnp.sum(jnp.where(oh0, 1.0, 0.0), axis=0, keepdims=True)
        cnt1 = jnp.sum(jnp.where(oh1, 1.0, 0.0), axis=0, keepdims=True)
        rank0 = jnp.sum(jnp.where(oh0, before[:, :ROUTER_LANES] + base, 0.0), axis=-1, keepdims=True)
        rank1 = jnp.sum(jnp.where(oh1, before[:, ROUTER_LANES:] + base + cnt0, 0.0), axis=-1, keepdims=True)
        base = base + cnt0 + cnt1

        rt = jnp.where(lane == 0, i0 - EXPERT_LANE0, 0.0)
        rt = jnp.where(lane == 1, i1 - EXPERT_LANE0, rt)
        rt = jnp.where(lane == 2, rank0, rt)
        rt = jnp.where(lane == 3, rank1, rt)
        rt = jnp.where(lane == 4, w0, rt)
        rt = jnp.where(lane == 5, w1, rt)
        rt_ref[rows, :] = rt
        rtt_ref[:, rows] = rt.T[0:ROUTE_FIELDS, :]
    base_ref[...] = base
    cnt_ref[...] = base


def _mix_route(x2d, attn2d, cb, u, ada3, conv_w, conv_b, woa, woc, g1, b1, wr, br, seq):
    t = x2d.shape[0]
    tm = TM_MIX
    tiles_per_seq = seq // tm
    hb = tm // BF16_SUBLANES
    n_halo = t // BF16_SUBLANES
    const = lambda i: (0, 0)
    return pl.pallas_call(
        functools.partial(_mix_route_kernel, seq=seq),
        out_shape=(
            jax.ShapeDtypeStruct((t, D_MODEL), F32),
            jax.ShapeDtypeStruct((PLANES, t, LANES), U32),
            jax.ShapeDtypeStruct((t, ROUTER_LANES), F32),
            jax.ShapeDtypeStruct((ROUTE_FIELDS, t), F32),
            jax.ShapeDtypeStruct((1, ROUTER_LANES), F32),
        ),
        grid=(t // tm,),
        in_specs=[
            pl.BlockSpec((tm, D_MODEL), lambda i: (i, 0)),
            pl.BlockSpec((tm, ATTN_WIDTH), lambda i: (i, 0)),
            pl.BlockSpec((tm, CONV_WIDTH), lambda i: (i, 0)),
            pl.BlockSpec((tm, CONV_WIDTH), lambda i: (i, 0)),
            pl.BlockSpec((BF16_SUBLANES, CONV_WIDTH), lambda i: (jnp.maximum(i * hb - 1, 0), 0)),
            pl.BlockSpec((BF16_SUBLANES, CONV_WIDTH), lambda i: (jnp.minimum((i + 1) * hb, n_halo - 1), 0)),
            pl.BlockSpec((1, 6, D_MODEL), lambda i: (i // tiles_per_seq, 0, 0)),
            pl.BlockSpec((CONV_K, CONV_WIDTH), const),
            pl.BlockSpec((1, CONV_WIDTH), const),
            pl.BlockSpec((ATTN_WIDTH, D_MODEL), const),
            pl.BlockSpec((CONV_WIDTH, D_MODEL), const),
            pl.BlockSpec((1, D_MODEL), const),
            pl.BlockSpec((1, D_MODEL), const),
            pl.BlockSpec((D_MODEL, 2 * ROUTER_LANES), const),
            pl.BlockSpec((1, ROUTER_LANES), const),
        ],
        out_specs=(
            pl.BlockSpec((tm, D_MODEL), lambda i: (i, 0)),
            pl.BlockSpec((PLANES, tm, LANES), lambda i: (0, i, 0)),
            pl.BlockSpec((tm, ROUTER_LANES), lambda i: (i, 0)),
            pl.BlockSpec((ROUTE_FIELDS, tm), lambda i: (0, i)),
            pl.BlockSpec((1, ROUTER_LANES), const),
        ),
        scratch_shapes=[pltpu.VMEM((1, ROUTER_LANES), F32)],
        compiler_params=_cparams(("arbitrary",)),
        name="mix_route",
    )(x2d, attn2d, cb, u, u, u, ada3, conv_w, conv_b, woa, woc, g1, b1, wr, br)


def _sc_mesh():
    return plsc.VectorSubcoreMesh(core_axis_name="core", subcore_axis_name="subcore")


def _sc_scatter_rows(src, idx, n_dst):
    m = idx.shape[0]
    w = SC_WINDOW
    n_src_blocks = src.shape[0] // w

    @pl.kernel(out_type=jax.ShapeDtypeStruct((n_dst, LANES), src.dtype), mesh=_sc_mesh(), scratch_types=[])
    def scatter(x_hbm, i_hbm, o_hbm):
        def body(x_vmem, i_vmem):
            pltpu.sync_copy(x_vmem, o_hbm.at[i_vmem.at[0]])

        pltpu.emit_pipeline(
            body, grid=(m // w,),
            in_specs=[pl.BlockSpec((w, LANES), index_map=lambda i: (i % n_src_blocks, 0)),
                      pl.BlockSpec((1, w), index_map=lambda i: (0, i))],
            out_specs=[],
            core_axis_name=("core", "subcore"),
            dimension_semantics=(pltpu.PARALLEL,),
        )(x_hbm, i_hbm)

    return scatter(src, idx.reshape(1, m))


def _sc_gather_rows(src, idx):
    m = idx.shape[0]
    w = SC_WINDOW

    @pl.kernel(out_type=jax.ShapeDtypeStruct((m, LANES), src.dtype), mesh=_sc_mesh(), scratch_types=[])
    def gather(x_hbm, i_hbm, o_hbm):
        def body(i_vmem, o_vmem):
            pltpu.sync_copy(x_hbm.at[i_vmem.at[0]], o_vmem)

        pltpu.emit_pipeline(
            body, grid=(m // w,),
            in_specs=[pl.BlockSpec((1, w), index_map=lambda i: (0, i))],
            out_specs=[pl.BlockSpec((w, LANES), index_map=lambda i: (i, 0))],
            core_axis_name=("core", "subcore"),
            dimension_semantics=(pltpu.PARALLEL,),
        )(i_hbm, o_hbm)

    return gather(src, idx.reshape(1, m))


def _to_planes(words):
    return [words[:, c * LANES:(c + 1) * LANES] for c in range(PLANES)]


def _experts_kernel(blk_exp_ref, n_used_ref, xs_ref, wg_ref, wu_ref, wd_ref, y_ref):
    del blk_exp_ref
    used = pl.program_id(0) < n_used_ref[0]

    @pl.when(jnp.logical_not(used))
    def _():
        y_ref[...] = jnp.zeros_like(y_ref)

    @pl.when(used)
    def _():
        x_lo, x_hi = _unpack_rows(jnp.concatenate([xs_ref[c] for c in range(PLANES)], axis=1))
        x_lo = x_lo.astype(BF16)
        x_hi = x_hi.astype(BF16)
        g = (_dot(x_lo, wg_ref[0, :HALF, :].astype(BF16))
             + _dot(x_hi, wg_ref[0, HALF:, :].astype(BF16)))
        up = (_dot(x_lo, wu_ref[0, :HALF, :].astype(BF16))
              + _dot(x_hi, wu_ref[0, HALF:, :].astype(BF16)))
        act = (g * jax.nn.sigmoid(g) * up).astype(BF16)
        for c, plane in enumerate(_to_planes(_pack_rows(_dot(act, wd_ref[0].astype(BF16))))):
            y_ref[c] = plane


def _experts(blk_exp, n_used, xs, wg, wu, wd):
    rows = xs.shape[1]
    blk = EXPERT_BLOCK
    return pl.pallas_call(
        _experts_kernel,
        out_shape=jax.ShapeDtypeStruct((PLANES, rows, LANES), U32),
        grid_spec=pltpu.PrefetchScalarGridSpec(
            num_scalar_prefetch=2,
            grid=(rows // blk,),
            in_specs=[
                pl.BlockSpec((PLANES, blk, LANES), lambda i, be, nu: (0, i, 0)),
                pl.BlockSpec((1, D_MODEL, D_FF_EXPERT), lambda i, be, nu: (be[i], 0, 0)),
                pl.BlockSpec((1, D_MODEL, D_FF_EXPERT), lambda i, be, nu: (be[i], 0, 0)),
                pl.BlockSpec((1, D_FF_EXPERT, D_MODEL), lambda i, be, nu: (be[i], 0, 0)),
            ],
            out_specs=pl.BlockSpec((PLANES, blk, LANES), lambda i, be, nu: (0, i, 0)),
        ),
        compiler_params=_cparams(("arbitrary",)),
        name="experts",
    )(blk_exp, n_used, xs, wg, wu, wd)


def _combine_kernel(x1_ref, rt_ref, ada_ref, g2_ref, b2_ref, yg_ref, o_ref):
    g_f = ada_ref[0][5:6, :]
    rt = rt_ref[...]
    w0 = rt[:, 4:5]
    w1 = rt[:, 5:6]
    y0_lo, y0_hi = _unpack_rows(jnp.concatenate([yg_ref[0, c] for c in range(PLANES)], axis=1))
    y1_lo, y1_hi = _unpack_rows(jnp.concatenate([yg_ref[1, c] for c in range(PLANES)], axis=1))
    f = jnp.concatenate([y0_lo * w0 + y1_lo * w1, y0_hi * w0 + y1_hi * w1], axis=1)
    z = DEEPNORM_ALPHA * x1_ref[...] + g_f * f
    o_ref[...] = _layernorm(z) * g2_ref[...] + b2_ref[...]


def _combine(x1, rt, ada3, g2, b2, yg, seq):
    t = x1.shape[0]
    tm = TM_COMBINE
    tiles_per_seq = seq // tm
    const = lambda i: (0, 0)
    return pl.pallas_call(
        _combine_kernel,
        out_shape=jax.ShapeDtypeStruct((t, D_MODEL), F32),
        grid=(t // tm,),
        in_specs=[
            pl.BlockSpec((tm, D_MODEL), lambda i: (i, 0)),
            pl.BlockSpec((tm, ROUTER_LANES), lambda i: (i, 0)),
            pl.BlockSpec((1, 6, D_MODEL), lambda i: (i // tiles_per_seq, 0, 0)),
            pl.BlockSpec((1, D_MODEL), const),
            pl.BlockSpec((1, D_MODEL), const),
            pl.BlockSpec((2, PLANES, tm, LANES), lambda i: (0, 0, i, 0)),
        ],
        out_specs=pl.BlockSpec((tm, D_MODEL), lambda i: (i, 0)),
        compiler_params=_cparams(("arbitrary",)),
        name="combine",
    )(x1, rt, ada3, g2, b2, yg)


def _rope_tables(seq):
    half = QK_HEAD_DIM // 2
    inv = 1.0 / (ROPE_THETA ** (jnp.arange(0, QK_HEAD_DIM, 2, dtype=F32) / QK_HEAD_DIM))
    ang = jnp.arange(seq, dtype=F32)[:, None] * inv[None, :]
    cos, sin = jnp.cos(ang), jnp.sin(ang)
    cos_t = jnp.tile(cos, (1, LANES // half))
    sin_t = jnp.tile(jnp.concatenate([-sin, sin], axis=1), (1, LANES // QK_HEAD_DIM))
    return cos_t, sin_t


def _routing_tables(rt_t, counts, n_rows_max):
    t = rt_t.shape[1]
    blk = EXPERT_BLOCK
    counts = counts[0, EXPERT_LANE0:EXPERT_LANE0 + N_EXPERTS].astype(jnp.int32)
    padded = ((counts + blk - 1) // blk) * blk
    pend = jnp.cumsum(padded)
    pstart = pend - padded
    e = rt_t[0:2].astype(jnp.int32)
    rank = rt_t[2:4].astype(jnp.int32)
    first_row = jnp.zeros_like(e)
    for j in range(N_EXPERTS):
        first_row = jnp.where(e == j, pstart[j], first_row)
    dest = first_row + rank
    plane0 = (jnp.arange(PLANES, dtype=jnp.int32) * n_rows_max)[None, :, None]
    row_idx = (dest[:, None, :] + plane0).reshape(2 * PLANES * t)
    blk_start = jnp.arange(n_rows_max // blk, dtype=jnp.int32) * blk
    blk_exp = jnp.minimum(jnp.sum(blk_start[:, None] >= pend[None, :], axis=1), N_EXPERTS - 1).astype(jnp.int32)
    n_used = (pend[-1:] // blk).astype(jnp.int32)
    return row_idx, blk_exp, n_used


def _encoder(x, c, p):
    b, seq, d = x.shape
    t = b * seq
    x2d = x.reshape(t, d)
    ada3 = _ada(c, p["w_ada_hi"], p["w_ada_lo"], p["b_ada"]).reshape(b, 6, d)
    cos_t, sin_t = _rope_tables(seq)
    q, k, v, cb, u = _in_proj(x2d, ada3, p["w_in"], cos_t, sin_t, seq)
    attn = _diff_attn(q.reshape(b, seq, 512), k.reshape(b, seq, 512), v.reshape(b, seq, 512),
                      p["lq1"], p["lk1"], p["lq2"], p["lk2"], p["subln_g"])
    x1, h2, rt, rt_t, counts = _mix_route(x2d, attn.reshape(t, 512), cb, u, ada3, p["conv_w"], p["conv_b"],
                                    p["woa"], p["woc"], p["ln1_g"], p["ln1_b"], p["wr"], p["br"], seq)
    n_rows_max = (2 * t // EXPERT_BLOCK + N_EXPERTS) * EXPERT_BLOCK
    row_idx, blk_exp, n_used = _routing_tables(rt_t, counts, n_rows_max)
    xs = _sc_scatter_rows(h2.reshape(PLANES * t, LANES), row_idx, PLANES * n_rows_max)
    y = _experts(blk_exp, n_used, xs.reshape(PLANES, n_rows_max, LANES), p["wg"], p["wu"], p["wd"])
    yg = _sc_gather_rows(y.reshape(PLANES * n_rows_max, LANES), row_idx)
    out = _combine(x1, rt, ada3, p["ln2_g"], p["ln2_b"], yg.reshape(2, PLANES, t, LANES), seq)
    return out.reshape(b, seq, d)


def kernel(x_prompt, x_sample, c_prompt, c_sample, w_ada, b_ada, w_in, lambda_q1, lambda_k1, lambda_q2, lambda_k2, attn_subln_g, conv_w, conv_b, w_out, ln1_g, ln1_b, router_group_w, router_group_b, router_expert_w, router_expert_b, expert_w_gate, expert_w_up, expert_w_down, ln2_g, ln2_b):
    l = 0
    w_ada_hi, w_ada_lo = _split_hi_lo(w_ada[l])
    wr = jnp.concatenate([router_group_w[l], router_expert_w[l]], axis=1)
    wr = jnp.pad(wr, ((0, 0), (0, ROUTER_LANES - wr.shape[1])))
    wr_hi, wr_lo = _split_hi_lo(wr)
    br = jnp.concatenate([router_group_b[l], router_expert_b[l]])
    br = jnp.pad(br, (0, ROUTER_LANES - br.shape[0])).reshape(1, ROUTER_LANES)
    w_out_bf = w_out[l].astype(BF16)
    p = {
        "w_ada_hi": w_ada_hi, "w_ada_lo": w_ada_lo, "b_ada": b_ada[l].reshape(1, -1),
        "w_in": w_in[l].astype(BF16),
        "lq1": lambda_q1[l].reshape(1, -1), "lk1": lambda_k1[l].reshape(1, -1),
        "lq2": lambda_q2[l].reshape(1, -1), "lk2": lambda_k2[l].reshape(1, -1),
        "subln_g": attn_subln_g[l].reshape(1, -1),
        "conv_w": conv_w[l], "conv_b": conv_b[l].reshape(1, -1),
        "woa": w_out_bf[:ATTN_WIDTH], "woc": w_out_bf[ATTN_WIDTH:],
        "ln1_g": ln1_g[l].reshape(1, -1), "ln1_b": ln1_b[l].reshape(1, -1),
        "wr": jnp.concatenate([wr_hi, wr_lo], axis=1), "br": br,
        "wg": expert_w_gate[l], "wu": expert_w_up[l], "wd": expert_w_down[l],
        "ln2_g": ln2_g[l].reshape(1, -1), "ln2_b": ln2_b[l].reshape(1, -1),
    }
    y_prompt = _encoder(x_prompt, c_prompt, p)
    y_sample = _encoder(x_sample, c_sample, p)
    return (y_prompt, y_sample)
```

```python
import functools
import math

import jax
import jax.numpy as jnp
from jax import lax
from jax.experimental import pallas as pl
from jax.experimental.pallas import tpu as pltpu
from jax.experimental.pallas import tpu_sc as plsc

D_MODEL = 1024
ATTN_WIDTH = 512
CONV_WIDTH = 512
QK_HEAD_DIM = 64
V_HEAD_DIM = 128
N_DIFF_HEADS = 4
IN_WIDTH = 3072
CONV_K = 3
ROPE_THETA = 10000.0
N_GROUPS = 4
EXPERTS_PER_GROUP = 8
N_EXPERTS = 32
D_FF_EXPERT = 512
LN_EPS = 1e-5
RMS_EPS = 1e-5
DEPTH = 1
DEEPNORM_ALPHA = (2.0 * DEPTH) ** 0.25
LAMBDA_INIT = 0.8 - 0.6 * math.exp(-0.3 * 0)

LANES = 128
BF16_SUBLANES = 16
VMEM_LIMIT = 48 * 1024 * 1024

TM_PROJ = 1024
TQ_MAX = 1024
Q_TILES_MIN = 4
TK = 512
TM_MIX = 1024
TM_COMBINE = 512
SC_WINDOW = 256
EXPERT_BLOCK = 512
ROUTER_LANES = 128
EXPERT_LANE0 = N_GROUPS
ROUTE_FIELDS = 8

BF16 = jnp.bfloat16
F32 = jnp.float32


def _cparams(sem):
    return pltpu.CompilerParams(dimension_semantics=sem, vmem_limit_bytes=VMEM_LIMIT)


def _layernorm(x):
    mu = jnp.mean(x, axis=-1, keepdims=True)
    xc = x - mu
    var = jnp.mean(xc * xc, axis=-1, keepdims=True)
    return xc * lax.rsqrt(var + LN_EPS)


def _split_hi_lo(a):
    hi = a.astype(BF16)
    lo = (a - hi.astype(F32)).astype(BF16)
    return hi, lo


def _dot(a, b):
    return jnp.dot(a, b, preferred_element_type=F32)


HALF = D_MODEL // 2
U32 = jnp.uint32
PLANES = HALF // LANES


def _pack_rows(x):
    lo = lax.bitcast_convert_type(x[:, :HALF].astype(BF16).astype(F32), U32)
    hi = lax.bitcast_convert_type(x[:, HALF:].astype(BF16).astype(F32), U32)
    return (lo >> 16) | hi


def _unpack_rows(w):
    lo = lax.bitcast_convert_type(w << 16, F32)
    hi = lax.bitcast_convert_type(w & jnp.uint32(0xFFFF0000), F32)
    return lo, hi


def _ada_kernel(c_ref, whi_ref, wlo_ref, b_ref, o_ref):
    c = c_ref[...]
    s = c * jax.nn.sigmoid(c)
    s_hi, s_lo = _split_hi_lo(s)
    acc = _dot(s_hi, whi_ref[...]) + _dot(s_lo, whi_ref[...]) + _dot(s_hi, wlo_ref[...])
    o_ref[...] = acc + b_ref[...]


def _ada(c, w_hi, w_lo, b):
    bsz = c.shape[0]
    n = w_hi.shape[1]
    tn = 1024
    return pl.pallas_call(
        _ada_kernel,
        out_shape=jax.ShapeDtypeStruct((bsz, n), F32),
        grid=(n // tn,),
        in_specs=[
            pl.BlockSpec((bsz, D_MODEL), lambda j: (0, 0)),
            pl.BlockSpec((D_MODEL, tn), lambda j: (0, j)),
            pl.BlockSpec((D_MODEL, tn), lambda j: (0, j)),
            pl.BlockSpec((1, tn), lambda j: (0, j)),
        ],
        out_specs=pl.BlockSpec((bsz, tn), lambda j: (0, j)),
        compiler_params=_cparams(("arbitrary",)),
        name="ada",
    )(c, w_hi, w_lo, b)


def _rope(x, cos_t, sin_t):
    lane = lax.broadcasted_iota(jnp.int32, x.shape, 1)
    upper = (lane & 32) != 0
    partner = jnp.where(upper, pltpu.roll(x, 32, axis=1), pltpu.roll(x, LANES - 32, axis=1))
    return x * cos_t + partner * sin_t


def _in_proj_kernel(x_ref, ada_ref, w_ref, cos_ref, sin_ref,
                    q_ref, k_ref, v_ref, cb_ref, u_ref):
    ada = ada_ref[0]
    sh_m = ada[0:1, :]
    sc_m = ada[1:2, :]
    qk_scale = QK_HEAD_DIM ** -0.5 * math.log2(math.e)
    hm = TM_PROJ // 2
    halves = [slice(h * hm, (h + 1) * hm) for h in range(2)]
    hs = [(_layernorm(x_ref[rows, :]) * (1.0 + sc_m) + sh_m).astype(BF16) for rows in halves]
    for rows, h in zip(halves, hs):
        cos_t = cos_ref[rows, :]
        sin_t = sin_ref[rows, :]
        q = _dot(h, w_ref[:, 0:512])
        k = _dot(h, w_ref[:, 512:1024])
        for j in range(4):
            lo = j * LANES
            q_ref[rows, lo:lo + LANES] = (_rope(q[:, lo:lo + LANES], cos_t, sin_t) * qk_scale).astype(BF16)
            k_ref[rows, lo:lo + LANES] = _rope(k[:, lo:lo + LANES], cos_t, sin_t).astype(BF16)
        v_ref[rows, :] = _dot(h, w_ref[:, 1024:1536]).astype(BF16)
        cb_ref[rows, :] = _dot(h, w_ref[:, 1536:2048]).astype(BF16)
        cc = _dot(h, w_ref[:, 2048:2560])
        ch = _dot(h, w_ref[:, 2560:3072])
        u_ref[rows, :] = (cc * ch).astype(BF16)


def _in_proj(x2d, ada3, w_in_bf, cos_t, sin_t, seq):
    t = x2d.shape[0]
    tm = TM_PROJ
    tiles_per_seq = seq // tm
    out = jax.ShapeDtypeStruct((t, 512), BF16)
    ospec = pl.BlockSpec((tm, 512), lambda i: (i, 0))
    return pl.pallas_call(
        _in_proj_kernel,
        out_shape=(out,) * 5,
        grid=(t // tm,),
        in_specs=[
            pl.BlockSpec((tm, D_MODEL), lambda i: (i, 0)),
            pl.BlockSpec((1, 6, D_MODEL), lambda i: (i // tiles_per_seq, 0, 0)),
            pl.BlockSpec((D_MODEL, IN_WIDTH), lambda i: (0, 0)),
            pl.BlockSpec((tm, LANES), lambda i: (i % tiles_per_seq, 0)),
            pl.BlockSpec((tm, LANES), lambda i: (i % tiles_per_seq, 0)),
        ],
        out_specs=(ospec,) * 5,
        compiler_params=_cparams(("arbitrary",)),
        name="in_proj",
    )(x2d, ada3, w_in_bf, cos_t, sin_t)


def _diff_attn_kernel(q_ref, k_ref, v_ref, lq1_ref, lk1_ref, lq2_ref, lk2_ref, g_ref, o_ref,
                      qq_ref, vx_ref, s_buf, m_ref, acc_ref, *, seq, tq, tk):
    nq = seq // tq
    nk = seq // tk
    n_lane_blocks = tk // LANES

    vx_ref[:, 0:V_HEAD_DIM] = v_ref[0]
    vx_ref[:, V_HEAD_DIM:2 * V_HEAD_DIM] = jnp.ones((seq, V_HEAD_DIM), BF16)
    lam = (jnp.exp(jnp.sum(lq1_ref[...] * lk1_ref[...], axis=-1, keepdims=True))
           - jnp.exp(jnp.sum(lq2_ref[...] * lk2_ref[...], axis=-1, keepdims=True)) + LAMBDA_INIT)

    def load_q(qt, qslot):
        q = q_ref[0, pl.ds(pl.multiple_of(qt * tq, tq), tq), :]
        lane = lax.broadcasted_iota(jnp.int32, q.shape, 1)
        zero = jnp.zeros_like(q)
        qq_ref[qslot, 0:tq, :] = jnp.where(lane < QK_HEAD_DIM, q, zero)
        qq_ref[qslot, tq:2 * tq, :] = jnp.where(lane >= QK_HEAD_DIM, q, zero)

    def scores(qslot, j, slot):
        start = pl.multiple_of(j * tk, tk)
        kc = k_ref[0, pl.ds(start, tk), :]
        s_buf[slot] = lax.dot_general(qq_ref[qslot], kc, (((1,), (1,)), ((), ())), preferred_element_type=F32)

    def softmax_pv(j, slot):
        blocks = [s_buf[slot, :, c * LANES:(c + 1) * LANES] for c in range(n_lane_blocks)]
        mb = blocks[0]
        for c in range(1, n_lane_blocks):
            mb = jnp.maximum(mb, blocks[c])
        m_old = m_ref[...]
        m_new = jnp.maximum(m_old, jnp.max(mb, axis=-1, keepdims=True))
        alpha = jnp.exp2(m_old - m_new)
        m_ref[...] = m_new
        p = jnp.concatenate([jnp.exp2(blk - m_new).astype(BF16) for blk in blocks], axis=1)
        start = pl.multiple_of(j * tk, tk)
        pv = _dot(p, vx_ref[pl.ds(start, tk), :])
        acc_ref[:, 0:V_HEAD_DIM] = alpha * acc_ref[:, 0:V_HEAD_DIM] + pv[:, 0:V_HEAD_DIM]
        acc_ref[:, V_HEAD_DIM:] = alpha * acc_ref[:, V_HEAD_DIM:] + pv[:, V_HEAD_DIM:]

    def q_tile(qt, qslot, has_next):
        m_ref[...] = jnp.full(m_ref.shape, -jnp.inf, F32)
        acc_ref[...] = jnp.zeros(acc_ref.shape, F32)

        def pair(jj, c):
            j = 2 * jj
            scores(qslot, j + 1, 1)
            softmax_pv(j, 0)
            scores(qslot, j + 2, 0)
            softmax_pv(j + 1, 1)
            return c

        lax.fori_loop(0, nk // 2 - 1, pair, 0)
        scores(qslot, nk - 1, 1)
        softmax_pv(nk - 2, 0)
        if has_next:
            load_q(qt + 1, 1 - qslot)
            scores(1 - qslot, 0, 0)
        softmax_pv(nk - 1, 1)

        o = acc_ref[:, 0:V_HEAD_DIM] / acc_ref[:, V_HEAD_DIM:]
        of = o[:tq] - lam * o[tq:]
        of = of * lax.rsqrt(jnp.mean(of * of, axis=-1, keepdims=True) + RMS_EPS)
        of = of * g_ref[...] * (1.0 - LAMBDA_INIT)
        o_ref[0, pl.ds(pl.multiple_of(qt * tq, tq), tq), :] = of.astype(BF16)

    load_q(0, 0)
    scores(0, 0, 0)

    def q_pair(i, c):
        q_tile(2 * i, 0, True)
        q_tile(2 * i + 1, 1, True)
        return c

    lax.fori_loop(0, nq // 2 - 1, q_pair, 0)
    q_tile(nq - 2, 0, True)
    q_tile(nq - 1, 1, False)


def _diff_attn(q, k, v, lq1, lk1, lq2, lk2, g):
    b, seq, _ = q.shape
    tk = TK
    tq = min(TQ_MAX, seq // Q_TILES_MIN)
    lam_spec = pl.BlockSpec((1, QK_HEAD_DIM), lambda bi, h: (0, 0))
    head_spec = pl.BlockSpec((1, seq, LANES), lambda bi, h: (bi, 0, h))
    return pl.pallas_call(
        functools.partial(_diff_attn_kernel, seq=seq, tq=tq, tk=tk),
        out_shape=jax.ShapeDtypeStruct((b, seq, ATTN_WIDTH), BF16),
        grid=(b, N_DIFF_HEADS),
        in_specs=[
            head_spec, head_spec, head_spec,
            lam_spec, lam_spec, lam_spec, lam_spec,
            pl.BlockSpec((1, V_HEAD_DIM), lambda bi, h: (0, 0)),
        ],
        out_specs=head_spec,
        scratch_shapes=[
            pltpu.VMEM((2, 2 * tq, LANES), BF16),
            pltpu.VMEM((seq, 2 * V_HEAD_DIM), BF16),
            pltpu.VMEM((2, 2 * tq, tk), F32),
            pltpu.VMEM((2 * tq, LANES), F32),
            pltpu.VMEM((2 * tq, 2 * V_HEAD_DIM), F32),
        ],
        compiler_params=_cparams(("arbitrary", "arbitrary")),
        name="diff_attn",
    )(q, k, v, lq1, lk1, lq2, lk2, g)


def _lane_min_index(mask, lane_f):
    return jnp.min(jnp.where(mask, lane_f, float(ROUTER_LANES)), axis=-1, keepdims=True)


def _mix_route_kernel(x_ref, attn_ref, cb_ref, u_ref, uprev_ref, unext_ref, ada_ref,
                      cw_ref, cbias_ref, woa_ref, woc_ref, g1_ref, b1_ref, wr_ref, br_ref,
                      x1_ref, h2_ref, rt_ref, rtt_ref, cnt_ref, base_ref, *, seq):
    i = pl.program_id(0)
    tm = TM_MIX

    @pl.when(i == 0)
    def _():
        base_ref[...] = jnp.zeros_like(base_ref)

    ada = ada_ref[0]
    g_m = ada[2:3, :]
    sh_f = ada[3:4, :]
    sc_f = ada[4:5, :]

    u = u_ref[...].astype(F32)
    row = lax.broadcasted_iota(jnp.int32, u.shape, 0)
    not_seq_start = ((i * tm) % seq != 0).astype(F32)
    not_seq_end = (((i + 1) * tm) % seq != 0).astype(F32)
    halo_prev = uprev_ref[...].astype(F32)[BF16_SUBLANES - 1:BF16_SUBLANES, :] * not_seq_start
    halo_next = unext_ref[...].astype(F32)[0:1, :] * not_seq_end
    u_prev = jnp.where(row == 0, halo_prev, pltpu.roll(u, 1, axis=0))
    u_next = jnp.where(row == tm - 1, halo_next, pltpu.roll(u, tm - 1, axis=0))
    cw = cw_ref[...]

    hm = tm // 2
    halves = [slice(h * hm, (h + 1) * hm) for h in range(2)]
    mixes = []
    for rows in halves:
        y = cbias_ref[...] + u_prev[rows, :] * cw[0:1, :]
        y = y + u[rows, :] * cw[1:2, :]
        y = y + u_next[rows, :] * cw[2:3, :]
        conv = (cb_ref[rows, :].astype(F32) * y).astype(BF16)
        mixes.append(_dot(attn_ref[rows, :], woa_ref[...]) + _dot(conv, woc_ref[...]))

    lane = lax.broadcasted_iota(jnp.int32, (hm, ROUTER_LANES), 1)
    lane_f = lane.astype(F32)
    neg = -jnp.inf
    tri = jnp.where(lax.broadcasted_iota(jnp.int32, (hm, hm), 1) < lax.broadcasted_iota(jnp.int32, (hm, hm), 0),
                    1.0, 0.0).astype(BF16)
    base = base_ref[...]
    for rows, mix in zip(halves, mixes):
        x1 = _layernorm(DEEPNORM_ALPHA * x_ref[rows, :] + g_m * mix) * g1_ref[...] + b1_ref[...]
        x1_ref[rows, :] = x1
        h2 = _layernorm(x1) * (1.0 + sc_f) + sh_f
        for c, plane in enumerate(_to_planes(_pack_rows(h2))):
            h2_ref[c, rows, :] = plane

        h_hi, h_lo = _split_hi_lo(h2)
        both = _dot(h_hi, wr_ref[...])
        logits = (both[:, :ROUTER_LANES] + both[:, ROUTER_LANES:]
                  + _dot(h_lo, wr_ref[:, :ROUTER_LANES]) + br_ref[...])

        lg = jnp.where(lane < N_GROUPS, logits, neg)
        lg_max = jnp.max(lg, axis=-1, keepdims=True)
        g_sel = _lane_min_index(lg == lg_max, lane_f)
        pg_sel = 1.0 / jnp.sum(jnp.exp(lg - lg_max), axis=-1, keepdims=True)

        first = EXPERT_LANE0 + EXPERTS_PER_GROUP * g_sel
        in_group = (lane_f >= first) & (lane_f < first + EXPERTS_PER_GROUP)
        le = jnp.where(in_group, logits, neg)
        l0 = jnp.max(le, axis=-1, keepdims=True)
        i0 = _lane_min_index(le == l0, lane_f)
        le2 = jnp.where(lane_f == i0, neg, le)
        l1 = jnp.max(le2, axis=-1, keepdims=True)
        i1 = _lane_min_index(le2 == l1, lane_f)
        t_exp = jnp.exp(l1 - l0)
        p0 = 1.0 / (1.0 + t_exp)
        w0 = pg_sel * p0
        w1 = pg_sel * (t_exp * p0)

        oh0 = lane_f == i0
        oh1 = lane_f == i1
        onehots = jnp.concatenate([jnp.where(oh0, 1.0, 0.0), jnp.where(oh1, 1.0, 0.0)], axis=1).astype(BF16)
        before = _dot(tri, onehots)
        cnt0 = jnp.sum(jnp.where(oh0, 1.0, 0.0), axis=0, keepdims=True)
        cnt1 = jnp.sum(jnp.where(oh1, 1.0, 0.0), axis=0, keepdims=True)
        rank0 = jnp.sum(jnp.where(oh0, before[:, :ROUTER_LANES] + base, 0.0), axis=-1, keepdims=True)
        rank1 = jnp.sum(jnp.where(oh1, before[:, ROUTER_LANES:] + base + cnt0, 0.0), axis=-1, keepdims=True)
        base = base + cnt0 + cnt1

        rt = jnp.where(lane == 0, i0 - EXPERT_LANE0, 0.0)
        rt = jnp.where(lane == 1, i1 - EXPERT_LANE0, rt)
        rt = jnp.where(lane == 2, rank0, rt)
        rt = jnp.where(lane == 3, rank1, rt)
        rt = jnp.where(lane == 4, w0, rt)
        rt = jnp.where(lane == 5, w1, rt)
        rt_ref[rows, :] = rt
        rtt_ref[:, rows] = rt.T[0:ROUTE_FIELDS, :]
    base_ref[...] = base
    cnt_ref[...] = base


def _mix_route(x2d, attn2d, cb, u, ada3, conv_w, conv_b, woa, woc, g1, b1, wr, br, seq):
    t = x2d.shape[0]
    tm = TM_MIX
    tiles_per_seq = seq // tm
    hb = tm // BF16_SUBLANES
    n_halo = t // BF16_SUBLANES
    const = lambda i: (0, 0)
    return pl.pallas_call(
        functools.partial(_mix_route_kernel, seq=seq),
        out_shape=(
            jax.ShapeDtypeStruct((t, D_MODEL), F32),
            jax.ShapeDtypeStruct((PLANES, t, LANES), U32),
            jax.ShapeDtypeStruct((t, ROUTER_LANES), F32),
            jax.ShapeDtypeStruct((ROUTE_FIELDS, t), F32),
            jax.ShapeDtypeStruct((1, ROUTER_LANES), F32),
        ),
        grid=(t // tm,),
        in_specs=[
            pl.BlockSpec((tm, D_MODEL), lambda i: (i, 0)),
            pl.BlockSpec((tm, ATTN_WIDTH), lambda i: (i, 0)),
            pl.BlockSpec((tm, CONV_WIDTH), lambda i: (i, 0)),
            pl.BlockSpec((tm, CONV_WIDTH), lambda i: (i, 0)),
            pl.BlockSpec((BF16_SUBLANES, CONV_WIDTH), lambda i: (jnp.maximum(i * hb - 1, 0), 0)),
            pl.BlockSpec((BF16_SUBLANES, CONV_WIDTH), lambda i: (jnp.minimum((i + 1) * hb, n_halo - 1), 0)),
            pl.BlockSpec((1, 6, D_MODEL), lambda i: (i // tiles_per_seq, 0, 0)),
            pl.BlockSpec((CONV_K, CONV_WIDTH), const),
            pl.BlockSpec((1, CONV_WIDTH), const),
            pl.BlockSpec((ATTN_WIDTH, D_MODEL), const),
            pl.BlockSpec((CONV_WIDTH, D_MODEL), const),
            pl.BlockSpec((1, D_MODEL), const),
            pl.BlockSpec((1, D_MODEL), const),
            pl.BlockSpec((D_MODEL, 2 * ROUTER_LANES), const),
            pl.BlockSpec((1, ROUTER_LANES), const),
        ],
        out_specs=(
            pl.BlockSpec((tm, D_MODEL), lambda i: (i, 0)),
            pl.BlockSpec((PLANES, tm, LANES), lambda i: (0, i, 0)),
            pl.BlockSpec((tm, ROUTER_LANES), lambda i: (i, 0)),
            pl.BlockSpec((ROUTE_FIELDS, tm), lambda i: (0, i)),
            pl.BlockSpec((1, ROUTER_LANES), const),
        ),
        scratch_shapes=[pltpu.VMEM((1, ROUTER_LANES), F32)],
        compiler_params=_cparams(("arbitrary",)),
        name="mix_route",
    )(x2d, attn2d, cb, u, u, u, ada3, conv_w, conv_b, woa, woc, g1, b1, wr, br)


def _sc_mesh():
    return plsc.VectorSubcoreMesh(core_axis_name="core", subcore_axis_name="subcore")


def _sc_scatter_rows(src, idx, n_dst):
    m = idx.shape[0]
    w = SC_WINDOW
    n_src_blocks = src.shape[0] // w

    @pl.kernel(out_type=jax.ShapeDtypeStruct((n_dst, LANES), src.dtype), mesh=_sc_mesh(), scratch_types=[])
    def scatter(x_hbm, i_hbm, o_hbm):
        def body(x_vmem, i_vmem):
            pltpu.sync_copy(x_vmem, o_hbm.at[i_vmem.at[0]])

        pltpu.emit_pipeline(
            body, grid=(m // w,),
            in_specs=[pl.BlockSpec((w, LANES), index_map=lambda i: (i % n_src_blocks, 0)),
                      pl.BlockSpec((1, w), index_map=lambda i: (0, i))],
            out_specs=[],
            core_axis_name=("core", "subcore"),
            dimension_semantics=(pltpu.PARALLEL,),
        )(x_hbm, i_hbm)

    return scatter(src, idx.reshape(1, m))


def _sc_gather_rows(src, idx):
    m = idx.shape[0]
    w = SC_WINDOW

    @pl.kernel(out_type=jax.ShapeDtypeStruct((m, LANES), src.dtype), mesh=_sc_mesh(), scratch_types=[])
    def gather(x_hbm, i_hbm, o_hbm):
        def body(i_vmem, o_vmem):
            pltpu.sync_copy(x_hbm.at[i_vmem.at[0]], o_vmem)

        pltpu.emit_pipeline(
            body, grid=(m // w,),
            in_specs=[pl.BlockSpec((1, w), index_map=lambda i: (0, i))],
            out_specs=[pl.BlockSpec((w, LANES), index_map=lambda i: (i, 0))],
            core_axis_name=("core", "subcore"),
            dimension_semantics=(pltpu.PARALLEL,),
        )(i_hbm, o_hbm)

    return gather(src, idx.reshape(1, m))


def _to_planes(words):
    return [words[:, c * LANES:(c + 1) * LANES] for c in range(PLANES)]


def _experts_kernel(blk_exp_ref, n_used_ref, xs_ref, wg_ref, wu_ref, wd_ref, y_ref):
    del blk_exp_ref
    used = pl.program_id(0) < n_used_ref[0]

    @pl.when(jnp.logical_not(used))
    def _():
        y_ref[...] = jnp.zeros_like(y_ref)

    @pl.when(used)
    def _():
        x_lo, x_hi = _unpack_rows(jnp.concatenate([xs_ref[c] for c in range(PLANES)], axis=1))
        x_lo = x_lo.astype(BF16)
        x_hi = x_hi.astype(BF16)
        g = (_dot(x_lo, wg_ref[0, :HALF, :].astype(BF16))
             + _dot(x_hi, wg_ref[0, HALF:, :].astype(BF16)))
        up = (_dot(x_lo, wu_ref[0, :HALF, :].astype(BF16))
              + _dot(x_hi, wu_ref[0, HALF:, :].astype(BF16)))
        act = (g * jax.nn.sigmoid(g) * up).astype(BF16)
        for c, plane in enumerate(_to_planes(_pack_rows(_dot(act, wd_ref[0].astype(BF16))))):
            y_ref[c] = plane


def _experts(blk_exp, n_used, xs, wg, wu, wd):
    rows = xs.shape[1]
    blk = EXPERT_BLOCK
    return pl.pallas_call(
        _experts_kernel,
        out_shape=jax.ShapeDtypeStruct((PLANES, rows, LANES), U32),
        grid_spec=pltpu.PrefetchScalarGridSpec(
            num_scalar_prefetch=2,
            grid=(rows // blk,),
            in_specs=[
                pl.BlockSpec((PLANES, blk, LANES), lambda i, be, nu: (0, i, 0)),
                pl.BlockSpec((1, D_MODEL, D_FF_EXPERT), lambda i, be, nu: (be[i], 0, 0)),
                pl.BlockSpec((1, D_MODEL, D_FF_EXPERT), lambda i, be, nu: (be[i], 0, 0)),
                pl.BlockSpec((1, D_FF_EXPERT, D_MODEL), lambda i, be, nu: (be[i], 0, 0)),
            ],
            out_specs=pl.BlockSpec((PLANES, blk, LANES), lambda i, be, nu: (0, i, 0)),
        ),
        compiler_params=_cparams(("arbitrary",)),
        name="experts",
    )(blk_exp, n_used, xs, wg, wu, wd)


def _combine_kernel(x1_ref, rt_ref, ada_ref, g2_ref, b2_ref, yg_ref, o_ref):
    g_f = ada_ref[0][5:6, :]
    rt = rt_ref[...]
    w0 = rt[:, 4:5]
    w1 = rt[:, 5:6]
    y0_lo, y0_hi = _unpack_rows(jnp.concatenate([yg_ref[0, c] for c in range(PLANES)], axis=1))
    y1_lo, y1_hi = _unpack_rows(jnp.concatenate([yg_ref[1, c] for c in range(PLANES)], axis=1))
    f = jnp.concatenate([y0_lo * w0 + y1_lo * w1, y0_hi * w0 + y1_hi * w1], axis=1)
    z = DEEPNORM_ALPHA * x1_ref[...] + g_f * f
    o_ref[...] = _layernorm(z) * g2_ref[...] + b2_ref[...]


def _combine(x1, rt, ada3, g2, b2, yg, seq):
    t = x1.shape[0]
    tm = TM_COMBINE
    tiles_per_seq = seq // tm
    const = lambda i: (0, 0)
    return pl.pallas_call(
        _combine_kernel,
        out_shape=jax.ShapeDtypeStruct((t, D_MODEL), F32),
        grid=(t // tm,),
        in_specs=[
            pl.BlockSpec((tm, D_MODEL), lambda i: (i, 0)),
            pl.BlockSpec((tm, ROUTER_LANES), lambda i: (i, 0)),
            pl.BlockSpec((1, 6, D_MODEL), lambda i: (i // tiles_per_seq, 0, 0)),
            pl.BlockSpec((1, D_MODEL), const),
            pl.BlockSpec((1, D_MODEL), const),
            pl.BlockSpec((2, PLANES, tm, LANES), lambda i: (0, 0, i, 0)),
        ],
        out_specs=pl.BlockSpec((tm, D_MODEL), lambda i: (i, 0)),
        compiler_params=_cparams(("arbitrary",)),
        name="combine",
    )(x1, rt, ada3, g2, b2, yg)


def _rope_tables(seq):
    half = QK_HEAD_DIM // 2
    inv = 1.0 / (ROPE_THETA ** (jnp.arange(0, QK_HEAD_DIM, 2, dtype=F32) / QK_HEAD_DIM))
    ang = jnp.arange(seq, dtype=F32)[:, None] * inv[None, :]
    cos, sin = jnp.cos(ang), jnp.sin(ang)
    cos_t = jnp.tile(cos, (1, LANES // half))
    sin_t = jnp.tile(jnp.concatenate([-sin, sin], axis=1), (1, LANES // QK_HEAD_DIM))
    return cos_t, sin_t


def _routing_tables(rt_t, counts, n_rows_max):
    t = rt_t.shape[1]
    blk = EXPERT_BLOCK
    counts = counts[0, EXPERT_LANE0:EXPERT_LANE0 + N_EXPERTS].astype(jnp.int32)
    padded = ((counts + blk - 1) // blk) * blk
    pend = jnp.cumsum(padded)
    pstart = pend - padded
    e = rt_t[0:2].astype(jnp.int32)
    rank = rt_t[2:4].astype(jnp.int32)
    first_row = jnp.zeros_like(e)
    for j in range(N_EXPERTS):
        first_row = jnp.where(e == j, pstart[j], first_row)
    dest = first_row + rank
    plane0 = (jnp.arange(PLANES, dtype=jnp.int32) * n_rows_max)[None, :, None]
    row_idx = (dest[:, None, :] + plane0).reshape(2 * PLANES * t)
    blk_start = jnp.arange(n_rows_max // blk, dtype=jnp.int32) * blk
    blk_exp = jnp.minimum(jnp.sum(blk_start[:, None] >= pend[None, :], axis=1), N_EXPERTS - 1).astype(jnp.int32)
    n_used = (pend[-1:] // blk).astype(jnp.int32)
    return row_idx, blk_exp, n_used


def _encoder(x, c, p):
    b, seq, d = x.shape
    t = b * seq
    x2d = x.reshape(t, d)
    ada3 = _ada(c, p["w_ada_hi"], p["w_ada_lo"], p["b_ada"]).reshape(b, 6, d)
    cos_t, sin_t = _rope_tables(seq)
    q, k, v, cb, u = _in_proj(x2d, ada3, p["w_in"], cos_t, sin_t, seq)
    attn = _diff_attn(q.reshape(b, seq, 512), k.reshape(b, seq, 512), v.reshape(b, seq, 512),
                      p["lq1"], p["lk1"], p["lq2"], p["lk2"], p["subln_g"])
    x1, h2, rt, rt_t, counts = _mix_route(x2d, attn.reshape(t, 512), cb, u, ada3, p["conv_w"], p["conv_b"],
                                    p["woa"], p["woc"], p["ln1_g"], p["ln1_b"], p["wr"], p["br"], seq)
    n_rows_max = (2 * t // EXPERT_BLOCK + N_EXPERTS) * EXPERT_BLOCK
    row_idx, blk_exp, n_used = _routing_tables(rt_t, counts, n_rows_max)
    xs = _sc_scatter_rows(h2.reshape(PLANES * t, LANES), row_idx, PLANES * n_rows_max)
    y = _experts(blk_exp, n_used, xs.reshape(PLANES, n_rows_max, LANES), p["wg"], p["wu"], p["wd"])
    yg = _sc_gather_rows(y.reshape(PLANES * n_rows_max, LANES), row_idx)
    out = _combine(x1, rt, ada3, p["ln2_g"], p["ln2_b"], yg.reshape(2, PLANES, t, LANES), seq)
    return out.reshape(b, seq, d)


def kernel(x_prompt, x_sample, c_prompt, c_sample, w_ada, b_ada, w_in, lambda_q1, lambda_k1, lambda_q2, lambda_k2, attn_subln_g, conv_w, conv_b, w_out, ln1_g, ln1_b, router_group_w, router_group_b, router_expert_w, router_expert_b, expert_w_gate, expert_w_up, expert_w_down, ln2_g, ln2_b):
    l = 0
    w_ada_hi, w_ada_lo = _split_hi_lo(w_ada[l])
    wr = jnp.concatenate([router_group_w[l], router_expert_w[l]], axis=1)
    wr = jnp.pad(wr, ((0, 0), (0, ROUTER_LANES - wr.shape[1])))
    wr_hi, wr_lo = _split_hi_lo(wr)
    br = jnp.concatenate([router_group_b[l], router_expert_b[l]])
    br = jnp.pad(br, (0, ROUTER_LANES - br.shape[0])).reshape(1, ROUTER_LANES)
    w_out_bf = w_out[l].astype(BF16)
    p = {
        "w_ada_hi": w_ada_hi, "w_ada_lo": w_ada_lo, "b_ada": b_ada[l].reshape(1, -1),
        "w_in": w_in[l].astype(BF16),
        "lq1": lambda_q1[l].reshape(1, -1), "lk1": lambda_k1[l].reshape(1, -1),
        "lq2": lambda_q2[l].reshape(1, -1), "lk2": lambda_k2[l].reshape(1, -1),
        "subln_g": attn_subln_g[l].reshape(1, -1),
        "conv_w": conv_w[l], "conv_b": conv_b[l].reshape(1, -1),
        "woa": w_out_bf[:ATTN_WIDTH], "woc": w_out_bf[ATTN_WIDTH:],
        "ln1_g": ln1_g[l].reshape(1, -1), "ln1_b": ln1_b[l].reshape(1, -1),
        "wr": jnp.concatenate([wr_hi, wr_lo], axis=1), "br": br,
        "wg": expert_w_gate[l], "wu": expert_w_up[l], "wd": expert_w_down[l],
        "ln2_g": ln2_g[l].reshape(1, -1), "ln2_b": ln2_b[l].reshape(1, -1),
    }
    y_prompt = _encoder(x_prompt, c_prompt, p)
    y_sample = _encoder(x_sample, c_sample, p)
    return (y_prompt, y_sample)
```

```python
import functools
import math

import jax
import jax.numpy as jnp
from jax import lax
from jax.experimental import pallas as pl
from jax.experimental.pallas import tpu as pltpu
from jax.experimental.pallas import tpu_sc as plsc

D_MODEL = 1024
ATTN_WIDTH = 512
CONV_WIDTH = 512
QK_HEAD_DIM = 64
V_HEAD_DIM = 128
N_DIFF_HEADS = 4
IN_WIDTH = 3072
CONV_K = 3
ROPE_THETA = 10000.0
N_GROUPS = 4
EXPERTS_PER_GROUP = 8
N_EXPERTS = 32
D_FF_EXPERT = 512
LN_EPS = 1e-5
RMS_EPS = 1e-5
DEPTH = 1
DEEPNORM_ALPHA = (2.0 * DEPTH) ** 0.25
LAMBDA_INIT = 0.8 - 0.6 * math.exp(-0.3 * 0)

LANES = 128
BF16_SUBLANES = 16
VMEM_LIMIT = 48 * 1024 * 1024

TM_PROJ = 1024
TQ_MAX = 1024
Q_TILES_MIN = 4
TK = 512
TM_MIX = 1024
TM_COMBINE = 1024
SC_WINDOW = 256
EXPERT_BLOCK = 1024
ROUTER_LANES = 128
EXPERT_LANE0 = N_GROUPS
ROUTE_FIELDS = 8

BF16 = jnp.bfloat16
F32 = jnp.float32


def _cparams(sem):
    return pltpu.CompilerParams(dimension_semantics=sem, vmem_limit_bytes=VMEM_LIMIT)


def _layernorm(x):
    mu = jnp.mean(x, axis=-1, keepdims=True)
    xc = x - mu
    var = jnp.mean(xc * xc, axis=-1, keepdims=True)
    return xc * lax.rsqrt(var + LN_EPS)


def _split_hi_lo(a):
    hi = a.astype(BF16)
    lo = (a - hi.astype(F32)).astype(BF16)
    return hi, lo


def _dot(a, b):
    return jnp.dot(a, b, preferred_element_type=F32)


HALF = D_MODEL // 2
U32 = jnp.uint32
PLANES = HALF // LANES


def _pack_rows(x):
    lo = lax.bitcast_convert_type(x[:, :HALF].astype(BF16).astype(F32), U32)
    hi = lax.bitcast_convert_type(x[:, HALF:].astype(BF16).astype(F32), U32)
    return (lo >> 16) | hi


def _unpack_rows(w):
    lo = lax.bitcast_convert_type(w << 16, F32)
    hi = lax.bitcast_convert_type(w & jnp.uint32(0xFFFF0000), F32)
    return lo, hi


def _ada_kernel(c_ref, whi_ref, wlo_ref, b_ref, o_ref):
    c = c_ref[...]
    s = c * jax.nn.sigmoid(c)
    s_hi, s_lo = _split_hi_lo(s)
    acc = _dot(s_hi, whi_ref[...]) + _dot(s_lo, whi_ref[...]) + _dot(s_hi, wlo_ref[...])
    o_ref[...] = acc + b_ref[...]


def _ada(c, w_hi, w_lo, b):
    bsz = c.shape[0]
    n = w_hi.shape[1]
    tn = 1024
    return pl.pallas_call(
        _ada_kernel,
        out_shape=jax.ShapeDtypeStruct((bsz, n), F32),
        grid=(n // tn,),
        in_specs=[
            pl.BlockSpec((bsz, D_MODEL), lambda j: (0, 0)),
            pl.BlockSpec((D_MODEL, tn), lambda j: (0, j)),
            pl.BlockSpec((D_MODEL, tn), lambda j: (0, j)),
            pl.BlockSpec((1, tn), lambda j: (0, j)),
        ],
        out_specs=pl.BlockSpec((bsz, tn), lambda j: (0, j)),
        compiler_params=_cparams(("arbitrary",)),
        name="ada",
    )(c, w_hi, w_lo, b)


def _rope(x, cos_t, sin_t):
    lane = lax.broadcasted_iota(jnp.int32, x.shape, 1)
    upper = (lane & 32) != 0
    partner = jnp.where(upper, pltpu.roll(x, 32, axis=1), pltpu.roll(x, LANES - 32, axis=1))
    return x * cos_t + partner * sin_t


def _in_proj_kernel(x_ref, ada_ref, w_ref, cos_ref, sin_ref,
                    q_ref, k_ref, v_ref, cb_ref, u_ref):
    ada = ada_ref[0]
    sh_m = ada[0:1, :]
    sc_m = ada[1:2, :]
    qk_scale = QK_HEAD_DIM ** -0.5 * math.log2(math.e)
    hm = TM_PROJ // 2
    halves = [slice(h * hm, (h + 1) * hm) for h in range(2)]
    hs = [(_layernorm(x_ref[rows, :]) * (1.0 + sc_m) + sh_m).astype(BF16) for rows in halves]
    for rows, h in zip(halves, hs):
        cos_t = cos_ref[rows, :]
        sin_t = sin_ref[rows, :]
        q = _dot(h, w_ref[:, 0:512])
        k = _dot(h, w_ref[:, 512:1024])
        for j in range(4):
            lo = j * LANES
            q_ref[rows, lo:lo + LANES] = (_rope(q[:, lo:lo + LANES], cos_t, sin_t) * qk_scale).astype(BF16)
            k_ref[rows, lo:lo + LANES] = _rope(k[:, lo:lo + LANES], cos_t, sin_t).astype(BF16)
        v_ref[rows, :] = _dot(h, w_ref[:, 1024:1536]).astype(BF16)
        cb_ref[rows, :] = _dot(h, w_ref[:, 1536:2048]).astype(BF16)
        cc = _dot(h, w_ref[:, 2048:2560])
        ch = _dot(h, w_ref[:, 2560:3072])
        u_ref[rows, :] = (cc * ch).astype(BF16)


def _in_proj(x2d, ada3, w_in_bf, cos_t, sin_t, seq):
    t = x2d.shape[0]
    tm = TM_PROJ
    tiles_per_seq = seq // tm
    out = jax.ShapeDtypeStruct((t, 512), BF16)
    ospec = pl.BlockSpec((tm, 512), lambda i: (i, 0))
    return pl.pallas_call(
        _in_proj_kernel,
        out_shape=(out,) * 5,
        grid=(t // tm,),
        in_specs=[
            pl.BlockSpec((tm, D_MODEL), lambda i: (i, 0)),
            pl.BlockSpec((1, 6, D_MODEL), lambda i: (i // tiles_per_seq, 0, 0)),
            pl.BlockSpec((D_MODEL, IN_WIDTH), lambda i: (0, 0)),
            pl.BlockSpec((tm, LANES), lambda i: (i % tiles_per_seq, 0)),
            pl.BlockSpec((tm, LANES), lambda i: (i % tiles_per_seq, 0)),
        ],
        out_specs=(ospec,) * 5,
        compiler_params=_cparams(("arbitrary",)),
        name="in_proj",
    )(x2d, ada3, w_in_bf, cos_t, sin_t)


def _diff_attn_kernel(q_ref, k_ref, v_ref, lq1_ref, lk1_ref, lq2_ref, lk2_ref, g_ref, o_ref,
                      qq_ref, vx_ref, s_buf, m_ref, acc_ref, *, seq, tq, tk):
    nq = seq // tq
    nk = seq // tk
    n_lane_blocks = tk // LANES

    vx_ref[:, 0:V_HEAD_DIM] = v_ref[0]
    vx_ref[:, V_HEAD_DIM:2 * V_HEAD_DIM] = jnp.ones((seq, V_HEAD_DIM), BF16)
    lam = (jnp.exp(jnp.sum(lq1_ref[...] * lk1_ref[...], axis=-1, keepdims=True))
           - jnp.exp(jnp.sum(lq2_ref[...] * lk2_ref[...], axis=-1, keepdims=True)) + LAMBDA_INIT)

    def load_q(qt, qslot):
        q = q_ref[0, pl.ds(pl.multiple_of(qt * tq, tq), tq), :]
        lane = lax.broadcasted_iota(jnp.int32, q.shape, 1)
        zero = jnp.zeros_like(q)
        qq_ref[qslot, 0:tq, :] = jnp.where(lane < QK_HEAD_DIM, q, zero)
        qq_ref[qslot, tq:2 * tq, :] = jnp.where(lane >= QK_HEAD_DIM, q, zero)

    def scores(qslot, j, slot):
        start = pl.multiple_of(j * tk, tk)
        kc = k_ref[0, pl.ds(start, tk), :]
        s_buf[slot] = lax.dot_general(qq_ref[qslot], kc, (((1,), (1,)), ((), ())), preferred_element_type=F32)

    def softmax_pv(j, slot):
        blocks = [s_buf[slot, :, c * LANES:(c + 1) * LANES] for c in range(n_lane_blocks)]
        mb = blocks[0]
        for c in range(1, n_lane_blocks):
            mb = jnp.maximum(mb, blocks[c])
        m_old = m_ref[...]
        m_new = jnp.maximum(m_old, jnp.max(mb, axis=-1, keepdims=True))
        alpha = jnp.exp2(m_old - m_new)
        m_ref[...] = m_new
        p = jnp.concatenate([jnp.exp2(blk - m_new).astype(BF16) for blk in blocks], axis=1)
        start = pl.multiple_of(j * tk, tk)
        pv = _dot(p, vx_ref[pl.ds(start, tk), :])
        acc_ref[:, 0:V_HEAD_DIM] = alpha * acc_ref[:, 0:V_HEAD_DIM] + pv[:, 0:V_HEAD_DIM]
        acc_ref[:, V_HEAD_DIM:] = alpha * acc_ref[:, V_HEAD_DIM:] + pv[:, V_HEAD_DIM:]

    def q_tile(qt, qslot, has_next):
        m_ref[...] = jnp.full(m_ref.shape, -jnp.inf, F32)
        acc_ref[...] = jnp.zeros(acc_ref.shape, F32)

        def pair(jj, c):
            j = 2 * jj
            scores(qslot, j + 1, 1)
            softmax_pv(j, 0)
            scores(qslot, j + 2, 0)
            softmax_pv(j + 1, 1)
            return c

        lax.fori_loop(0, nk // 2 - 1, pair, 0)
        scores(qslot, nk - 1, 1)
        softmax_pv(nk - 2, 0)
        if has_next:
            load_q(qt + 1, 1 - qslot)
            scores(1 - qslot, 0, 0)
        softmax_pv(nk - 1, 1)

        o = acc_ref[:, 0:V_HEAD_DIM] / acc_ref[:, V_HEAD_DIM:]
        of = o[:tq] - lam * o[tq:]
        of = of * lax.rsqrt(jnp.mean(of * of, axis=-1, keepdims=True) + RMS_EPS)
        of = of * g_ref[...] * (1.0 - LAMBDA_INIT)
        o_ref[0, pl.ds(pl.multiple_of(qt * tq, tq), tq), :] = of.astype(BF16)

    load_q(0, 0)
    scores(0, 0, 0)

    def q_pair(i, c):
        q_tile(2 * i, 0, True)
        q_tile(2 * i + 1, 1, True)
        return c

    lax.fori_loop(0, nq // 2 - 1, q_pair, 0)
    q_tile(nq - 2, 0, True)
    q_tile(nq - 1, 1, False)


def _diff_attn(q, k, v, lq1, lk1, lq2, lk2, g):
    b, seq, _ = q.shape
    tk = TK
    tq = min(TQ_MAX, seq // Q_TILES_MIN)
    lam_spec = pl.BlockSpec((1, QK_HEAD_DIM), lambda bi, h: (0, 0))
    head_spec = pl.BlockSpec((1, seq, LANES), lambda bi, h: (bi, 0, h))
    return pl.pallas_call(
        functools.partial(_diff_attn_kernel, seq=seq, tq=tq, tk=tk),
        out_shape=jax.ShapeDtypeStruct((b, seq, ATTN_WIDTH), BF16),
        grid=(b, N_DIFF_HEADS),
        in_specs=[
            head_spec, head_spec, head_spec,
            lam_spec, lam_spec, lam_spec, lam_spec,
            pl.BlockSpec((1, V_HEAD_DIM), lambda bi, h: (0, 0)),
        ],
        out_specs=head_spec,
        scratch_shapes=[
            pltpu.VMEM((2, 2 * tq, LANES), BF16),
            pltpu.VMEM((seq, 2 * V_HEAD_DIM), BF16),
            pltpu.VMEM((2, 2 * tq, tk), F32),
            pltpu.VMEM((2 * tq, LANES), F32),
            pltpu.VMEM((2 * tq, 2 * V_HEAD_DIM), F32),
        ],
        compiler_params=_cparams(("arbitrary", "arbitrary")),
        name="diff_attn",
    )(q, k, v, lq1, lk1, lq2, lk2, g)


def _lane_min_index(mask, lane_f):
    return jnp.min(jnp.where(mask, lane_f, float(ROUTER_LANES)), axis=-1, keepdims=True)


def _mix_route_kernel(x_ref, attn_ref, cb_ref, u_ref, uprev_ref, unext_ref, ada_ref,
                      cw_ref, cbias_ref, woa_ref, woc_ref, g1_ref, b1_ref, wr_ref, br_ref,
                      x1_ref, h2_ref, rt_ref, rtt_ref, cnt_ref, base_ref, *, seq):
    i = pl.program_id(0)
    tm = TM_MIX

    @pl.when(i == 0)
    def _():
        base_ref[...] = jnp.zeros_like(base_ref)

    ada = ada_ref[0]
    g_m = ada[2:3, :]
    sh_f = ada[3:4, :]
    sc_f = ada[4:5, :]

    u = u_ref[...].astype(F32)
    row = lax.broadcasted_iota(jnp.int32, u.shape, 0)
    not_seq_start = ((i * tm) % seq != 0).astype(F32)
    not_seq_end = (((i + 1) * tm) % seq != 0).astype(F32)
    halo_prev = uprev_ref[...].astype(F32)[BF16_SUBLANES - 1:BF16_SUBLANES, :] * not_seq_start
    halo_next = unext_ref[...].astype(F32)[0:1, :] * not_seq_end
    u_prev = jnp.where(row == 0, halo_prev, pltpu.roll(u, 1, axis=0))
    u_next = jnp.where(row == tm - 1, halo_next, pltpu.roll(u, tm - 1, axis=0))
    cw = cw_ref[...]

    hm = tm // 2
    halves = [slice(h * hm, (h + 1) * hm) for h in range(2)]
    mixes = []
    for rows in halves:
        y = cbias_ref[...] + u_prev[rows, :] * cw[0:1, :]
        y = y + u[rows, :] * cw[1:2, :]
        y = y + u_next[rows, :] * cw[2:3, :]
        conv = (cb_ref[rows, :].astype(F32) * y).astype(BF16)
        mixes.append(_dot(attn_ref[rows, :], woa_ref[...]) + _dot(conv, woc_ref[...]))

    lane = lax.broadcasted_iota(jnp.int32, (hm, ROUTER_LANES), 1)
    lane_f = lane.astype(F32)
    neg = -jnp.inf
    tri = jnp.where(lax.broadcasted_iota(jnp.int32, (hm, hm), 1) < lax.broadcasted_iota(jnp.int32, (hm, hm), 0),
                    1.0, 0.0).astype(BF16)
    base = base_ref[...]
    for rows, mix in zip(halves, mixes):
        x1 = _layernorm(DEEPNORM_ALPHA * x_ref[rows, :] + g_m * mix) * g1_ref[...] + b1_ref[...]
        x1_ref[rows, :] = x1
        h2 = _layernorm(x1) * (1.0 + sc_f) + sh_f
        for c, plane in enumerate(_to_planes(_pack_rows(h2))):
            h2_ref[c, rows, :] = plane

        h_hi, h_lo = _split_hi_lo(h2)
        both = _dot(h_hi, wr_ref[...])
        logits = (both[:, :ROUTER_LANES] + both[:, ROUTER_LANES:]
                  + _dot(h_lo, wr_ref[:, :ROUTER_LANES]) + br_ref[...])

        lg = jnp.where(lane < N_GROUPS, logits, neg)
        lg_max = jnp.max(lg, axis=-1, keepdims=True)
        g_sel = _lane_min_index(lg == lg_max, lane_f)
        pg_sel = 1.0 / jnp.sum(jnp.exp(lg - lg_max), axis=-1, keepdims=True)

        first = EXPERT_LANE0 + EXPERTS_PER_GROUP * g_sel
        in_group = (lane_f >= first) & (lane_f < first + EXPERTS_PER_GROUP)
        le = jnp.where(in_group, logits, neg)
        l0 = jnp.max(le, axis=-1, keepdims=True)
        i0 = _lane_min_index(le == l0, lane_f)
        le2 = jnp.where(lane_f == i0, neg, le)
        l1 = jnp.max(le2, axis=-1, keepdims=True)
        i1 = _lane_min_index(le2 == l1, lane_f)
        t_exp = jnp.exp(l1 - l0)
        p0 = 1.0 / (1.0 + t_exp)
        w0 = pg_sel * p0
        w1 = pg_sel * (t_exp * p0)

        oh0 = lane_f == i0
        oh1 = lane_f == i1
        onehots = jnp.concatenate([jnp.where(oh0, 1.0, 0.0), jnp.where(oh1, 1.0, 0.0)], axis=1).astype(BF16)
        before = _dot(tri, onehots)
        cnt0 = jnp.sum(jnp.where(oh0, 1.0, 0.0), axis=0, keepdims=True)
        cnt1 = jnp.sum(jnp.where(oh1, 1.0, 0.0), axis=0, keepdims=True)
        rank0 = jnp.sum(jnp.where(oh0, before[:, :ROUTER_LANES] + base, 0.0), axis=-1, keepdims=True)
        rank1 = jnp.sum(jnp.where(oh1, before[:, ROUTER_LANES:] + base + cnt0, 0.0), axis=-1, keepdims=True)
        base = base + cnt0 + cnt1

        rt = jnp.where(lane == 0, i0 - EXPERT_LANE0, 0.0)
        rt = jnp.where(lane == 1, i1 - EXPERT_LANE0, rt)
        rt = jnp.where(lane == 2, rank0, rt)
        rt = jnp.where(lane == 3, rank1, rt)
        rt = jnp.where(lane == 4, w0, rt)
        rt = jnp.where(lane == 5, w1, rt)
        rt_ref[rows, :] = rt
        rtt_ref[:, rows] = rt.T[0:ROUTE_FIELDS, :]
    base_ref[...] = base
    cnt_ref[...] = base


def _mix_route(x2d, attn2d, cb, u, ada3, conv_w, conv_b, woa, woc, g1, b1, wr, br, seq):
    t = x2d.shape[0]
    tm = TM_MIX
    tiles_per_seq = seq // tm
    hb = tm // BF16_SUBLANES
    n_halo = t // BF16_SUBLANES
    const = lambda i: (0, 0)
    return pl.pallas_call(
        functools.partial(_mix_route_kernel, seq=seq),
        out_shape=(
            jax.ShapeDtypeStruct((t, D_MODEL), F32),
            jax.ShapeDtypeStruct((PLANES, t, LANES), U32),
            jax.ShapeDtypeStruct((t, ROUTER_LANES), F32),
            jax.ShapeDtypeStruct((ROUTE_FIELDS, t), F32),
            jax.ShapeDtypeStruct((1, ROUTER_LANES), F32),
        ),
        grid=(t // tm,),
        in_specs=[
            pl.BlockSpec((tm, D_MODEL), lambda i: (i, 0)),
            pl.BlockSpec((tm, ATTN_WIDTH), lambda i: (i, 0)),
            pl.BlockSpec((tm, CONV_WIDTH), lambda i: (i, 0)),
            pl.BlockSpec((tm, CONV_WIDTH), lambda i: (i, 0)),
            pl.BlockSpec((BF16_SUBLANES, CONV_WIDTH), lambda i: (jnp.maximum(i * hb - 1, 0), 0)),
            pl.BlockSpec((BF16_SUBLANES, CONV_WIDTH), lambda i: (jnp.minimum((i + 1) * hb, n_halo - 1), 0)),
            pl.BlockSpec((1, 6, D_MODEL), lambda i: (i // tiles_per_seq, 0, 0)),
            pl.BlockSpec((CONV_K, CONV_WIDTH), const),
            pl.BlockSpec((1, CONV_WIDTH), const),
            pl.BlockSpec((ATTN_WIDTH, D_MODEL), const),
            pl.BlockSpec((CONV_WIDTH, D_MODEL), const),
            pl.BlockSpec((1, D_MODEL), const),
            pl.BlockSpec((1, D_MODEL), const),
            pl.BlockSpec((D_MODEL, 2 * ROUTER_LANES), const),
            pl.BlockSpec((1, ROUTER_LANES), const),
        ],
        out_specs=(
            pl.BlockSpec((tm, D_MODEL), lambda i: (i, 0)),
            pl.BlockSpec((PLANES, tm, LANES), lambda i: (0, i, 0)),
            pl.BlockSpec((tm, ROUTER_LANES), lambda i: (i, 0)),
            pl.BlockSpec((ROUTE_FIELDS, tm), lambda i: (0, i)),
            pl.BlockSpec((1, ROUTER_LANES), const),
        ),
        scratch_shapes=[pltpu.VMEM((1, ROUTER_LANES), F32)],
        compiler_params=_cparams(("arbitrary",)),
        name="mix_route",
    )(x2d, attn2d, cb, u, u, u, ada3, conv_w, conv_b, woa, woc, g1, b1, wr, br)


def _sc_mesh():
    return plsc.VectorSubcoreMesh(core_axis_name="core", subcore_axis_name="subcore")


def _sc_scatter_rows(src, idx, n_dst):
    m = idx.shape[0]
    w = SC_WINDOW
    n_src_blocks = src.shape[0] // w

    @pl.kernel(out_type=jax.ShapeDtypeStruct((n_dst, LANES), src.dtype), mesh=_sc_mesh(), scratch_types=[])
    def scatter(x_hbm, i_hbm, o_hbm):
        def body(x_vmem, i_vmem):
            pltpu.sync_copy(x_vmem, o_hbm.at[i_vmem.at[0]])

        pltpu.emit_pipeline(
            body, grid=(m // w,),
            in_specs=[pl.BlockSpec((w, LANES), index_map=lambda i: (i % n_src_blocks, 0)),
                      pl.BlockSpec((1, w), index_map=lambda i: (0, i))],
            out_specs=[],
            core_axis_name=("core", "subcore"),
            dimension_semantics=(pltpu.PARALLEL,),
        )(x_hbm, i_hbm)

    return scatter(src, idx.reshape(1, m))


def _sc_gather_rows(src, idx):
    m = idx.shape[0]
    w = SC_WINDOW

    @pl.kernel(out_type=jax.ShapeDtypeStruct((m, LANES), src.dtype), mesh=_sc_mesh(), scratch_types=[])
    def gather(x_hbm, i_hbm, o_hbm):
        def body(i_vmem, o_vmem):
            pltpu.sync_copy(x_hbm.at[i_vmem.at[0]], o_vmem)

        pltpu.emit_pipeline(
            body, grid=(m // w,),
            in_specs=[pl.BlockSpec((1, w), index_map=lambda i: (0, i))],
            out_specs=[pl.BlockSpec((w, LANES), index_map=lambda i: (i, 0))],
            core_axis_name=("core", "subcore"),
            dimension_semantics=(pltpu.PARALLEL,),
        )(i_hbm, o_hbm)

    return gather(src, idx.reshape(1, m))


def _to_planes(words):
    return [words[:, c * LANES:(c + 1) * LANES] for c in range(PLANES)]


def _experts_kernel(blk_exp_ref, n_used_ref, xs_ref, wg_ref, wu_ref, wd_ref, y_ref):
    del blk_exp_ref
    used = pl.program_id(0) < n_used_ref[0]

    @pl.when(jnp.logical_not(used))
    def _():
        y_ref[...] = jnp.zeros_like(y_ref)

    @pl.when(used)
    def _():
        x_lo, x_hi = _unpack_rows(jnp.concatenate([xs_ref[c] for c in range(PLANES)], axis=1))
        x_lo = x_lo.astype(BF16)
        x_hi = x_hi.astype(BF16)
        g = (_dot(x_lo, wg_ref[0, :HALF, :].astype(BF16))
             + _dot(x_hi, wg_ref[0, HALF:, :].astype(BF16)))
        up = (_dot(x_lo, wu_ref[0, :HALF, :].astype(BF16))
              + _dot(x_hi, wu_ref[0, HALF:, :].astype(BF16)))
        act = (g * jax.nn.sigmoid(g) * up).astype(BF16)
        for c, plane in enumerate(_to_planes(_pack_rows(_dot(act, wd_ref[0].astype(BF16))))):
            y_ref[c] = plane


def _experts(blk_exp, n_used, xs, wg, wu, wd):
    rows = xs.shape[1]
    blk = EXPERT_BLOCK
    return pl.pallas_call(
        _experts_kernel,
        out_shape=jax.ShapeDtypeStruct((PLANES, rows, LANES), U32),
        grid_spec=pltpu.PrefetchScalarGridSpec(
            num_scalar_prefetch=2,
            grid=(rows // blk,),
            in_specs=[
                pl.BlockSpec((PLANES, blk, LANES), lambda i, be, nu: (0, i, 0)),
                pl.BlockSpec((1, D_MODEL, D_FF_EXPERT), lambda i, be, nu: (be[i], 0, 0)),
                pl.BlockSpec((1, D_MODEL, D_FF_EXPERT), lambda i, be, nu: (be[i], 0, 0)),
                pl.BlockSpec((1, D_FF_EXPERT, D_MODEL), lambda i, be, nu: (be[i], 0, 0)),
            ],
            out_specs=pl.BlockSpec((PLANES, blk, LANES), lambda i, be, nu: (0, i, 0)),
        ),
        compiler_params=_cparams(("arbitrary",)),
        name="experts",
    )(blk_exp, n_used, xs, wg, wu, wd)


def _combine_kernel(x1_ref, rt_ref, ada_ref, g2_ref, b2_ref, yg_ref, o_ref):
    g_f = ada_ref[0][5:6, :]
    rt = rt_ref[...]
    w0 = rt[:, 4:5]
    w1 = rt[:, 5:6]
    y0_lo, y0_hi = _unpack_rows(jnp.concatenate([yg_ref[0, c] for c in range(PLANES)], axis=1))
    y1_lo, y1_hi = _unpack_rows(jnp.concatenate([yg_ref[1, c] for c in range(PLANES)], axis=1))
    f = jnp.concatenate([y0_lo * w0 + y1_lo * w1, y0_hi * w0 + y1_hi * w1], axis=1)
    z = DEEPNORM_ALPHA * x1_ref[...] + g_f * f
    o_ref[...] = _layernorm(z) * g2_ref[...] + b2_ref[...]


def _combine(x1, rt, ada3, g2, b2, yg, seq):
    t = x1.shape[0]
    tm = TM_COMBINE
    tiles_per_seq = seq // tm
    const = lambda i: (0, 0)
    return pl.pallas_call(
        _combine_kernel,
        out_shape=jax.ShapeDtypeStruct((t, D_MODEL), F32),
        grid=(t // tm,),
        in_specs=[
            pl.BlockSpec((tm, D_MODEL), lambda i: (i, 0)),
            pl.BlockSpec((tm, ROUTER_LANES), lambda i: (i, 0)),
            pl.BlockSpec((1, 6, D_MODEL), lambda i: (i // tiles_per_seq, 0, 0)),
            pl.BlockSpec((1, D_MODEL), const),
            pl.BlockSpec((1, D_MODEL), const),
            pl.BlockSpec((2, PLANES, tm, LANES), lambda i: (0, 0, i, 0)),
        ],
        out_specs=pl.BlockSpec((tm, D_MODEL), lambda i: (i, 0)),
        compiler_params=_cparams(("arbitrary",)),
        name="combine",
    )(x1, rt, ada3, g2, b2, yg)


def _rope_tables(seq):
    half = QK_HEAD_DIM // 2
    inv = 1.0 / (ROPE_THETA ** (jnp.arange(0, QK_HEAD_DIM, 2, dtype=F32) / QK_HEAD_DIM))
    ang = jnp.arange(seq, dtype=F32)[:, None] * inv[None, :]
    cos, sin = jnp.cos(ang), jnp.sin(ang)
    cos_t = jnp.tile(cos, (1, LANES // half))
    sin_t = jnp.tile(jnp.concatenate([-sin, sin], axis=1), (1, LANES // QK_HEAD_DIM))
    return cos_t, sin_t


def _routing_tables(rt_t, counts, n_rows_max):
    t = rt_t.shape[1]
    blk = EXPERT_BLOCK
    counts = counts[0, EXPERT_LANE0:EXPERT_LANE0 + N_EXPERTS].astype(jnp.int32)
    padded = ((counts + blk - 1) // blk) * blk
    pend = jnp.cumsum(padded)
    pstart = pend - padded
    e = rt_t[0:2].astype(jnp.int32)
    rank = rt_t[2:4].astype(jnp.int32)
    first_row = jnp.zeros_like(e)
    for j in range(N_EXPERTS):
        first_row = jnp.where(e == j, pstart[j], first_row)
    dest = first_row + rank
    plane0 = (jnp.arange(PLANES, dtype=jnp.int32) * n_rows_max)[None, :, None]
    row_idx = (dest[:, None, :] + plane0).reshape(2 * PLANES * t)
    blk_start = jnp.arange(n_rows_max // blk, dtype=jnp.int32) * blk
    blk_exp = jnp.minimum(jnp.sum(blk_start[:, None] >= pend[None, :], axis=1), N_EXPERTS - 1).astype(jnp.int32)
    n_used = (pend[-1:] // blk).astype(jnp.int32)
    return row_idx, blk_exp, n_used


def _encoder(x, c, p):
    b, seq, d = x.shape
    t = b * seq
    x2d = x.reshape(t, d)
    ada3 = _ada(c, p["w_ada_hi"], p["w_ada_lo"], p["b_ada"]).reshape(b, 6, d)
    cos_t, sin_t = _rope_tables(seq)
    q, k, v, cb, u = _in_proj(x2d, ada3, p["w_in"], cos_t, sin_t, seq)
    attn = _diff_attn(q.reshape(b, seq, 512), k.reshape(b, seq, 512), v.reshape(b, seq, 512),
                      p["lq1"], p["lk1"], p["lq2"], p["lk2"], p["subln_g"])
    x1, h2, rt, rt_t, counts = _mix_route(x2d, attn.reshape(t, 512), cb, u, ada3, p["conv_w"], p["conv_b"],
                                    p["woa"], p["woc"], p["ln1_g"], p["ln1_b"], p["wr"], p["br"], seq)
    n_rows_max = (2 * t // EXPERT_BLOCK + N_EXPERTS) * EXPERT_BLOCK
    row_idx, blk_exp, n_used = _routing_tables(rt_t, counts, n_rows_max)
    xs = _sc_scatter_rows(h2.reshape(PLANES * t, LANES), row_idx, PLANES * n_rows_max)
    y = _experts(blk_exp, n_used, xs.reshape(PLANES, n_rows_max, LANES), p["wg"], p["wu"], p["wd"])
    yg = _sc_gather_rows(y.reshape(PLANES * n_rows_max, LANES), row_idx)
    out = _combine(x1, rt, ada3, p["ln2_g"], p["ln2_b"], yg.reshape(2, PLANES, t, LANES), seq)
    return out.reshape(b, seq, d)


def kernel(x_prompt, x_sample, c_prompt, c_sample, w_ada, b_ada, w_in, lambda_q1, lambda_k1, lambda_q2, lambda_k2, attn_subln_g, conv_w, conv_b, w_out, ln1_g, ln1_b, router_group_w, router_group_b, router_expert_w, router_expert_b, expert_w_gate, expert_w_up, expert_w_down, ln2_g, ln2_b):
    l = 0
    w_ada_hi, w_ada_lo = _split_hi_lo(w_ada[l])
    wr = jnp.concatenate([router_group_w[l], router_expert_w[l]], axis=1)
    wr = jnp.pad(wr, ((0, 0), (0, ROUTER_LANES - wr.shape[1])))
    wr_hi, wr_lo = _split_hi_lo(wr)
    br = jnp.concatenate([router_group_b[l], router_expert_b[l]])
    br = jnp.pad(br, (0, ROUTER_LANES - br.shape[0])).reshape(1, ROUTER_LANES)
    w_out_bf = w_out[l].astype(BF16)
    p = {
        "w_ada_hi": w_ada_hi, "w_ada_lo": w_ada_lo, "b_ada": b_ada[l].reshape(1, -1),
        "w_in": w_in[l].astype(BF16),
        "lq1": lambda_q1[l].reshape(1, -1), "lk1": lambda_k1[l].reshape(1, -1),
        "lq2": lambda_q2[l].reshape(1, -1), "lk2": lambda_k2[l].reshape(1, -1),
        "subln_g": attn_subln_g[l].reshape(1, -1),
        "conv_w": conv_w[l], "conv_b": conv_b[l].reshape(1, -1),
        "woa": w_out_bf[:ATTN_WIDTH], "woc": w_out_bf[ATTN_WIDTH:],
        "ln1_g": ln1_g[l].reshape(1, -1), "ln1_b": ln1_b[l].reshape(1, -1),
        "wr": jnp.concatenate([wr_hi, wr_lo], axis=1), "br": br,
        "wg": expert_w_gate[l], "wu": expert_w_up[l], "wd": expert_w_down[l],
        "ln2_g": ln2_g[l].reshape(1, -1), "ln2_b": ln2_b[l].reshape(1, -1),
    }
    y_prompt = _encoder(x_prompt, c_prompt, p)
    y_sample = _encoder(x_sample, c_sample, p)
    return (y_prompt, y_sample)
```

```python
import functools
import math

import jax
import jax.numpy as jnp
from jax import lax
from jax.experimental import pallas as pl
from jax.experimental.pallas import tpu as pltpu
from jax.experimental.pallas import tpu_sc as plsc

D_MODEL = 1024
ATTN_WIDTH = 512
CONV_WIDTH = 512
QK_HEAD_DIM = 64
V_HEAD_DIM = 128
N_DIFF_HEADS = 4
IN_WIDTH = 3072
CONV_K = 3
ROPE_THETA = 10000.0
N_GROUPS = 4
EXPERTS_PER_GROUP = 8
N_EXPERTS = 32
D_FF_EXPERT = 512
LN_EPS = 1e-5
RMS_EPS = 1e-5
DEPTH = 1
DEEPNORM_ALPHA = (2.0 * DEPTH) ** 0.25
LAMBDA_INIT = 0.8 - 0.6 * math.exp(-0.3 * 0)

LANES = 128
BF16_SUBLANES = 16
VMEM_LIMIT = 48 * 1024 * 1024

TM_PROJ = 1024
TQ_MAX = 1024
Q_TILES_MIN = 4
TK = 512
TM_MIX = 1024
TM_COMBINE = 1024
SC_WINDOW = 256
EXPERT_BLOCK = 1024
ROUTER_LANES = 128
EXPERT_LANE0 = N_GROUPS
ROUTE_FIELDS = 8

BF16 = jnp.bfloat16
F32 = jnp.float32


def _cparams(sem):
    return pltpu.CompilerParams(dimension_semantics=sem, vmem_limit_bytes=VMEM_LIMIT)


def _layernorm(x):
    mu = jnp.mean(x, axis=-1, keepdims=True)
    xc = x - mu
    var = jnp.mean(xc * xc, axis=-1, keepdims=True)
    return xc * lax.rsqrt(var + LN_EPS)


def _split_hi_lo(a):
    hi = a.astype(BF16)
    lo = (a - hi.astype(F32)).astype(BF16)
    return hi, lo


def _dot(a, b):
    return jnp.dot(a, b, preferred_element_type=F32)


HALF = D_MODEL // 2
U32 = jnp.uint32
PLANES = HALF // LANES


def _pack_rows(x):
    lo = lax.bitcast_convert_type(x[:, :HALF].astype(BF16).astype(F32), U32)
    hi = lax.bitcast_convert_type(x[:, HALF:].astype(BF16).astype(F32), U32)
    return (lo >> 16) | hi


def _unpack_rows(w):
    lo = lax.bitcast_convert_type(w << 16, F32)
    hi = lax.bitcast_convert_type(w & jnp.uint32(0xFFFF0000), F32)
    return lo, hi


def _ada_kernel(c_ref, w_ref, b_ref, o_ref):
    c = c_ref[...]
    s = c * jax.nn.sigmoid(c)
    s_hi, s_lo = _split_hi_lo(s)
    w_hi, w_lo = _split_hi_lo(w_ref[...])
    acc = _dot(s_hi, w_hi) + _dot(s_lo, w_hi) + _dot(s_hi, w_lo)
    o_ref[...] = acc + b_ref[...]


def _ada(c, w, b):
    bsz = c.shape[0]
    n = w.shape[1]
    tn = 1024
    return pl.pallas_call(
        _ada_kernel,
        out_shape=jax.ShapeDtypeStruct((bsz, n), F32),
        grid=(n // tn,),
        in_specs=[
            pl.BlockSpec((bsz, D_MODEL), lambda j: (0, 0)),
            pl.BlockSpec((D_MODEL, tn), lambda j: (0, j)),
            pl.BlockSpec((1, tn), lambda j: (0, j)),
        ],
        out_specs=pl.BlockSpec((bsz, tn), lambda j: (0, j)),
        compiler_params=_cparams(("arbitrary",)),
        name="ada",
    )(c, w, b)


def _rope(x, cos_t, sin_t):
    lane = lax.broadcasted_iota(jnp.int32, x.shape, 1)
    upper = (lane & 32) != 0
    partner = jnp.where(upper, pltpu.roll(x, 32, axis=1), pltpu.roll(x, LANES - 32, axis=1))
    return x * cos_t + partner * sin_t


def _in_proj_kernel(x_ref, ada_ref, w_ref, cos_ref, sin_ref,
                    q_ref, k_ref, v_ref, cb_ref, u_ref):
    ada = ada_ref[0]
    sh_m = ada[0:1, :]
    sc_m = ada[1:2, :]
    qk_scale = QK_HEAD_DIM ** -0.5 * math.log2(math.e)
    hm = TM_PROJ // 2
    halves = [slice(h * hm, (h + 1) * hm) for h in range(2)]
    hs = [(_layernorm(x_ref[rows, :]) * (1.0 + sc_m) + sh_m).astype(BF16) for rows in halves]
    for rows, h in zip(halves, hs):
        cos_t = cos_ref[rows, :]
        sin_t = sin_ref[rows, :]
        q = _dot(h, w_ref[:, 0:512])
        k = _dot(h, w_ref[:, 512:1024])
        for j in range(4):
            lo = j * LANES
            q_ref[rows, lo:lo + LANES] = (_rope(q[:, lo:lo + LANES], cos_t, sin_t) * qk_scale).astype(BF16)
            k_ref[rows, lo:lo + LANES] = _rope(k[:, lo:lo + LANES], cos_t, sin_t).astype(BF16)
        v_ref[rows, :] = _dot(h, w_ref[:, 1024:1536]).astype(BF16)
        cb_ref[rows, :] = _dot(h, w_ref[:, 1536:2048]).astype(BF16)
        cc = _dot(h, w_ref[:, 2048:2560])
        ch = _dot(h, w_ref[:, 2560:3072])
        u_ref[rows, :] = (cc * ch).astype(BF16)


def _in_proj(x2d, ada3, w_in_bf, cos_t, sin_t, seq):
    t = x2d.shape[0]
    tm = TM_PROJ
    tiles_per_seq = seq // tm
    out = jax.ShapeDtypeStruct((t, 512), BF16)
    ospec = pl.BlockSpec((tm, 512), lambda i: (i, 0))
    return pl.pallas_call(
        _in_proj_kernel,
        out_shape=(out,) * 5,
        grid=(t // tm,),
        in_specs=[
            pl.BlockSpec((tm, D_MODEL), lambda i: (i, 0)),
            pl.BlockSpec((1, 6, D_MODEL), lambda i: (i // tiles_per_seq, 0, 0)),
            pl.BlockSpec((D_MODEL, IN_WIDTH), lambda i: (0, 0)),
            pl.BlockSpec((tm, LANES), lambda i: (i % tiles_per_seq, 0)),
            pl.BlockSpec((tm, LANES), lambda i: (i % tiles_per_seq, 0)),
        ],
        out_specs=(ospec,) * 5,
        compiler_params=_cparams(("arbitrary",)),
        name="in_proj",
    )(x2d, ada3, w_in_bf, cos_t, sin_t)


def _diff_attn_kernel(q_ref, k_ref, v_ref, lq1_ref, lk1_ref, lq2_ref, lk2_ref, g_ref, o_ref,
                      qq_ref, vx_ref, s_buf, m_ref, acc_ref, *, seq, tq, tk):
    nq = seq // tq
    nk = seq // tk
    n_lane_blocks = tk // LANES

    vx_ref[:, 0:V_HEAD_DIM] = v_ref[0]
    vx_ref[:, V_HEAD_DIM:2 * V_HEAD_DIM] = jnp.ones((seq, V_HEAD_DIM), BF16)
    lam = (jnp.exp(jnp.sum(lq1_ref[...] * lk1_ref[...], axis=-1, keepdims=True))
           - jnp.exp(jnp.sum(lq2_ref[...] * lk2_ref[...], axis=-1, keepdims=True)) + LAMBDA_INIT)

    def load_q(qt, qslot):
        q = q_ref[0, pl.ds(pl.multiple_of(qt * tq, tq), tq), :]
        lane = lax.broadcasted_iota(jnp.int32, q.shape, 1)
        zero = jnp.zeros_like(q)
        qq_ref[qslot, 0:tq, :] = jnp.where(lane < QK_HEAD_DIM, q, zero)
        qq_ref[qslot, tq:2 * tq, :] = jnp.where(lane >= QK_HEAD_DIM, q, zero)

    def scores(qslot, j, slot):
        start = pl.multiple_of(j * tk, tk)
        kc = k_ref[0, pl.ds(start, tk), :]
        s_buf[slot] = lax.dot_general(qq_ref[qslot], kc, (((1,), (1,)), ((), ())), preferred_element_type=F32)

    def softmax_pv(j, slot):
        blocks = [s_buf[slot, :, c * LANES:(c + 1) * LANES] for c in range(n_lane_blocks)]
        mb = blocks[0]
        for c in range(1, n_lane_blocks):
            mb = jnp.maximum(mb, blocks[c])
        m_old = m_ref[...]
        m_new = jnp.maximum(m_old, jnp.max(mb, axis=-1, keepdims=True))
        alpha = jnp.exp2(m_old - m_new)
        m_ref[...] = m_new
        p = jnp.concatenate([jnp.exp2(blk - m_new).astype(BF16) for blk in blocks], axis=1)
        start = pl.multiple_of(j * tk, tk)
        pv = _dot(p, vx_ref[pl.ds(start, tk), :])
        acc_ref[:, 0:V_HEAD_DIM] = alpha * acc_ref[:, 0:V_HEAD_DIM] + pv[:, 0:V_HEAD_DIM]
        acc_ref[:, V_HEAD_DIM:] = alpha * acc_ref[:, V_HEAD_DIM:] + pv[:, V_HEAD_DIM:]

    def q_tile(qt, qslot, has_next):
        m_ref[...] = jnp.full(m_ref.shape, -jnp.inf, F32)
        acc_ref[...] = jnp.zeros(acc_ref.shape, F32)

        def pair(jj, c):
            j = 2 * jj
            scores(qslot, j + 1, 1)
            softmax_pv(j, 0)
            scores(qslot, j + 2, 0)
            softmax_pv(j + 1, 1)
            return c

        lax.fori_loop(0, nk // 2 - 1, pair, 0)
        scores(qslot, nk - 1, 1)
        softmax_pv(nk - 2, 0)
        if has_next:
            load_q(qt + 1, 1 - qslot)
            scores(1 - qslot, 0, 0)
        softmax_pv(nk - 1, 1)

        o = acc_ref[:, 0:V_HEAD_DIM] / acc_ref[:, V_HEAD_DIM:]
        of = o[:tq] - lam * o[tq:]
        of = of * lax.rsqrt(jnp.mean(of * of, axis=-1, keepdims=True) + RMS_EPS)
        of = of * g_ref[...] * (1.0 - LAMBDA_INIT)
        o_ref[0, pl.ds(pl.multiple_of(qt * tq, tq), tq), :] = of.astype(BF16)

    load_q(0, 0)
    scores(0, 0, 0)

    def q_pair(i, c):
        q_tile(2 * i, 0, True)
        q_tile(2 * i + 1, 1, True)
        return c

    lax.fori_loop(0, nq // 2 - 1, q_pair, 0)
    q_tile(nq - 2, 0, True)
    q_tile(nq - 1, 1, False)


def _diff_attn(q, k, v, lq1, lk1, lq2, lk2, g):
    b, seq, _ = q.shape
    tk = TK
    tq = min(TQ_MAX, seq // Q_TILES_MIN)
    lam_spec = pl.BlockSpec((1, QK_HEAD_DIM), lambda bi, h: (0, 0))
    head_spec = pl.BlockSpec((1, seq, LANES), lambda bi, h: (bi, 0, h))
    return pl.pallas_call(
        functools.partial(_diff_attn_kernel, seq=seq, tq=tq, tk=tk),
        out_shape=jax.ShapeDtypeStruct((b, seq, ATTN_WIDTH), BF16),
        grid=(b, N_DIFF_HEADS),
        in_specs=[
            head_spec, head_spec, head_spec,
            lam_spec, lam_spec, lam_spec, lam_spec,
            pl.BlockSpec((1, V_HEAD_DIM), lambda bi, h: (0, 0)),
        ],
        out_specs=head_spec,
        scratch_shapes=[
            pltpu.VMEM((2, 2 * tq, LANES), BF16),
            pltpu.VMEM((seq, 2 * V_HEAD_DIM), BF16),
            pltpu.VMEM((2, 2 * tq, tk), F32),
            pltpu.VMEM((2 * tq, LANES), F32),
            pltpu.VMEM((2 * tq, 2 * V_HEAD_DIM), F32),
        ],
        compiler_params=_cparams(("arbitrary", "arbitrary")),
        name="diff_attn",
    )(q, k, v, lq1, lk1, lq2, lk2, g)


def _lane_min_index(mask, lane_f):
    return jnp.min(jnp.where(mask, lane_f, float(ROUTER_LANES)), axis=-1, keepdims=True)


def _mix_route_kernel(x_ref, attn_ref, cb_ref, u_ref, uprev_ref, unext_ref, ada_ref,
                      cw_ref, cbias_ref, woa_ref, woc_ref, g1_ref, b1_ref, wr_ref, br_ref,
                      x1_ref, h2_ref, rt_ref, rtt_ref, cnt_ref, base_ref, *, seq):
    i = pl.program_id(0)
    tm = TM_MIX

    @pl.when(i == 0)
    def _():
        base_ref[...] = jnp.zeros_like(base_ref)

    ada = ada_ref[0]
    g_m = ada[2:3, :]
    sh_f = ada[3:4, :]
    sc_f = ada[4:5, :]

    u = u_ref[...].astype(F32)
    row = lax.broadcasted_iota(jnp.int32, u.shape, 0)
    not_seq_start = ((i * tm) % seq != 0).astype(F32)
    not_seq_end = (((i + 1) * tm) % seq != 0).astype(F32)
    halo_prev = uprev_ref[...].astype(F32)[BF16_SUBLANES - 1:BF16_SUBLANES, :] * not_seq_start
    halo_next = unext_ref[...].astype(F32)[0:1, :] * not_seq_end
    u_prev = jnp.where(row == 0, halo_prev, pltpu.roll(u, 1, axis=0))
    u_next = jnp.where(row == tm - 1, halo_next, pltpu.roll(u, tm - 1, axis=0))
    cw = cw_ref[...]

    hm = tm // 2
    halves = [slice(h * hm, (h + 1) * hm) for h in range(2)]
    mixes = []
    for rows in halves:
        y = cbias_ref[...] + u_prev[rows, :] * cw[0:1, :]
        y = y + u[rows, :] * cw[1:2, :]
        y = y + u_next[rows, :] * cw[2:3, :]
        conv = (cb_ref[rows, :].astype(F32) * y).astype(BF16)
        mixes.append(_dot(attn_ref[rows, :], woa_ref[...]) + _dot(conv, woc_ref[...]))

    lane = lax.broadcasted_iota(jnp.int32, (hm, ROUTER_LANES), 1)
    lane_f = lane.astype(F32)
    neg = -jnp.inf
    tri = jnp.where(lax.broadcasted_iota(jnp.int32, (hm, hm), 1) < lax.broadcasted_iota(jnp.int32, (hm, hm), 0),
                    1.0, 0.0).astype(BF16)
    base = base_ref[...]
    for rows, mix in zip(halves, mixes):
        x1 = _layernorm(DEEPNORM_ALPHA * x_ref[rows, :] + g_m * mix) * g1_ref[...] + b1_ref[...]
        x1_ref[rows, :] = x1
        h2 = _layernorm(x1) * (1.0 + sc_f) + sh_f
        for c, plane in enumerate(_to_planes(_pack_rows(h2))):
            h2_ref[c, rows, :] = plane

        h_hi, h_lo = _split_hi_lo(h2)
        both = _dot(h_hi, wr_ref[...])
        logits = (both[:, :ROUTER_LANES] + both[:, ROUTER_LANES:]
                  + _dot(h_lo, wr_ref[:, :ROUTER_LANES]) + br_ref[...])

        lg = jnp.where(lane < N_GROUPS, logits, neg)
        lg_max = jnp.max(lg, axis=-1, keepdims=True)
        g_sel = _lane_min_index(lg == lg_max, lane_f)
        pg_sel = 1.0 / jnp.sum(jnp.exp(lg - lg_max), axis=-1, keepdims=True)

        first = EXPERT_LANE0 + EXPERTS_PER_GROUP * g_sel
        in_group = (lane_f >= first) & (lane_f < first + EXPERTS_PER_GROUP)
        le = jnp.where(in_group, logits, neg)
        l0 = jnp.max(le, axis=-1, keepdims=True)
        i0 = _lane_min_index(le == l0, lane_f)
        le2 = jnp.where(lane_f == i0, neg, le)
        l1 = jnp.max(le2, axis=-1, keepdims=True)
        i1 = _lane_min_index(le2 == l1, lane_f)
        t_exp = jnp.exp(l1 - l0)
        p0 = 1.0 / (1.0 + t_exp)
        w0 = pg_sel * p0
        w1 = pg_sel * (t_exp * p0)

        oh0 = lane_f == i0
        oh1 = lane_f == i1
        onehots = jnp.concatenate([jnp.where(oh0, 1.0, 0.0), jnp.where(oh1, 1.0, 0.0)], axis=1).astype(BF16)
        before = _dot(tri, onehots)
        cnt0 = jnp.sum(jnp.where(oh0, 1.0, 0.0), axis=0, keepdims=True)
        cnt1 = jnp.sum(jnp.where(oh1, 1.0, 0.0), axis=0, keepdims=True)
        rank0 = jnp.sum(jnp.where(oh0, before[:, :ROUTER_LANES] + base, 0.0), axis=-1, keepdims=True)
        rank1 = jnp.sum(jnp.where(oh1, before[:, ROUTER_LANES:] + base + cnt0, 0.0), axis=-1, keepdims=True)
        base = base + cnt0 + cnt1

        rt = jnp.where(lane == 0, i0 - EXPERT_LANE0, 0.0)
        rt = jnp.where(lane == 1, i1 - EXPERT_LANE0, rt)
        rt = jnp.where(lane == 2, rank0, rt)
        rt = jnp.where(lane == 3, rank1, rt)
        rt = jnp.where(lane == 4, w0, rt)
        rt = jnp.where(lane == 5, w1, rt)
        rt_ref[rows, :] = rt
        rtt_ref[:, rows] = rt.T[0:ROUTE_FIELDS, :]
    base_ref[...] = base
    cnt_ref[...] = base


def _mix_route(x2d, attn2d, cb, u, ada3, conv_w, conv_b, woa, woc, g1, b1, wr, br, seq):
    t = x2d.shape[0]
    tm = TM_MIX
    tiles_per_seq = seq // tm
    hb = tm // BF16_SUBLANES
    n_halo = t // BF16_SUBLANES
    const = lambda i: (0, 0)
    return pl.pallas_call(
        functools.partial(_mix_route_kernel, seq=seq),
        out_shape=(
            jax.ShapeDtypeStruct((t, D_MODEL), F32),
            jax.ShapeDtypeStruct((PLANES, t, LANES), U32),
            jax.ShapeDtypeStruct((t, ROUTER_LANES), F32),
            jax.ShapeDtypeStruct((ROUTE_FIELDS, t), F32),
            jax.ShapeDtypeStruct((1, ROUTER_LANES), F32),
        ),
        grid=(t // tm,),
        in_specs=[
            pl.BlockSpec((tm, D_MODEL), lambda i: (i, 0)),
            pl.BlockSpec((tm, ATTN_WIDTH), lambda i: (i, 0)),
            pl.BlockSpec((tm, CONV_WIDTH), lambda i: (i, 0)),
            pl.BlockSpec((tm, CONV_WIDTH), lambda i: (i, 0)),
            pl.BlockSpec((BF16_SUBLANES, CONV_WIDTH), lambda i: (jnp.maximum(i * hb - 1, 0), 0)),
            pl.BlockSpec((BF16_SUBLANES, CONV_WIDTH), lambda i: (jnp.minimum((i + 1) * hb, n_halo - 1), 0)),
            pl.BlockSpec((1, 6, D_MODEL), lambda i: (i // tiles_per_seq, 0, 0)),
            pl.BlockSpec((CONV_K, CONV_WIDTH), const),
            pl.BlockSpec((1, CONV_WIDTH), const),
            pl.BlockSpec((ATTN_WIDTH, D_MODEL), const),
            pl.BlockSpec((CONV_WIDTH, D_MODEL), const),
            pl.BlockSpec((1, D_MODEL), const),
            pl.BlockSpec((1, D_MODEL), const),
            pl.BlockSpec((D_MODEL, 2 * ROUTER_LANES), const),
            pl.BlockSpec((1, ROUTER_LANES), const),
        ],
        out_specs=(
            pl.BlockSpec((tm, D_MODEL), lambda i: (i, 0)),
            pl.BlockSpec((PLANES, tm, LANES), lambda i: (0, i, 0)),
            pl.BlockSpec((tm, ROUTER_LANES), lambda i: (i, 0)),
            pl.BlockSpec((ROUTE_FIELDS, tm), lambda i: (0, i)),
            pl.BlockSpec((1, ROUTER_LANES), const),
        ),
        scratch_shapes=[pltpu.VMEM((1, ROUTER_LANES), F32)],
        compiler_params=_cparams(("arbitrary",)),
        name="mix_route",
    )(x2d, attn2d, cb, u, u, u, ada3, conv_w, conv_b, woa, woc, g1, b1, wr, br)


def _sc_mesh():
    return plsc.VectorSubcoreMesh(core_axis_name="core", subcore_axis_name="subcore")


def _sc_scatter_rows(src, idx, n_dst):
    m = idx.shape[0]
    w = SC_WINDOW
    n_src_blocks = src.shape[0] // w

    @pl.kernel(out_type=jax.ShapeDtypeStruct((n_dst, LANES), src.dtype), mesh=_sc_mesh(), scratch_types=[])
    def scatter(x_hbm, i_hbm, o_hbm):
        def body(x_vmem, i_vmem):
            pltpu.sync_copy(x_vmem, o_hbm.at[i_vmem.at[0]])

        pltpu.emit_pipeline(
            body, grid=(m // w,),
            in_specs=[pl.BlockSpec((w, LANES), index_map=lambda i: (i % n_src_blocks, 0)),
                      pl.BlockSpec((1, w), index_map=lambda i: (0, i))],
            out_specs=[],
            core_axis_name=("core", "subcore"),
            dimension_semantics=(pltpu.PARALLEL,),
        )(x_hbm, i_hbm)

    return scatter(src, idx.reshape(1, m))


def _sc_gather_rows(src, idx):
    m = idx.shape[0]
    w = SC_WINDOW

    @pl.kernel(out_type=jax.ShapeDtypeStruct((m, LANES), src.dtype), mesh=_sc_mesh(), scratch_types=[])
    def gather(x_hbm, i_hbm, o_hbm):
        def body(i_vmem, o_vmem):
            pltpu.sync_copy(x_hbm.at[i_vmem.at[0]], o_vmem)

        pltpu.emit_pipeline(
            body, grid=(m // w,),
            in_specs=[pl.BlockSpec((1, w), index_map=lambda i: (0, i))],
            out_specs=[pl.BlockSpec((w, LANES), index_map=lambda i: (i, 0))],
            core_axis_name=("core", "subcore"),
            dimension_semantics=(pltpu.PARALLEL,),
        )(i_hbm, o_hbm)

    return gather(src, idx.reshape(1, m))


def _to_planes(words):
    return [words[:, c * LANES:(c + 1) * LANES] for c in range(PLANES)]


def _experts_kernel(blk_exp_ref, n_used_ref, xs_ref, wg_ref, wu_ref, wd_ref, y_ref):
    del blk_exp_ref

    @pl.when(pl.program_id(0) < n_used_ref[0])
    def _():
        x_lo, x_hi = _unpack_rows(jnp.concatenate([xs_ref[c] for c in range(PLANES)], axis=1))
        x_lo = x_lo.astype(BF16)
        x_hi = x_hi.astype(BF16)
        g = (_dot(x_lo, wg_ref[0, :HALF, :].astype(BF16))
             + _dot(x_hi, wg_ref[0, HALF:, :].astype(BF16)))
        up = (_dot(x_lo, wu_ref[0, :HALF, :].astype(BF16))
              + _dot(x_hi, wu_ref[0, HALF:, :].astype(BF16)))
        act = (g * jax.nn.sigmoid(g) * up).astype(BF16)
        for c, plane in enumerate(_to_planes(_pack_rows(_dot(act, wd_ref[0].astype(BF16))))):
            y_ref[c] = plane


def _experts(blk_exp, n_used, xs, wg, wu, wd):
    rows = xs.shape[1]
    blk = EXPERT_BLOCK

    def last_used(i, nu):
        return jnp.minimum(i, nu[0] - 1)

    return pl.pallas_call(
        _experts_kernel,
        out_shape=jax.ShapeDtypeStruct((PLANES, rows, LANES), U32),
        grid_spec=pltpu.PrefetchScalarGridSpec(
            num_scalar_prefetch=2,
            grid=(rows // blk,),
            in_specs=[
                pl.BlockSpec((PLANES, blk, LANES), lambda i, be, nu: (0, last_used(i, nu), 0)),
                pl.BlockSpec((1, D_MODEL, D_FF_EXPERT), lambda i, be, nu: (be[last_used(i, nu)], 0, 0)),
                pl.BlockSpec((1, D_MODEL, D_FF_EXPERT), lambda i, be, nu: (be[last_used(i, nu)], 0, 0)),
                pl.BlockSpec((1, D_FF_EXPERT, D_MODEL), lambda i, be, nu: (be[last_used(i, nu)], 0, 0)),
            ],
            out_specs=pl.BlockSpec((PLANES, blk, LANES), lambda i, be, nu: (0, last_used(i, nu), 0)),
        ),
        compiler_params=_cparams(("arbitrary",)),
        name="experts",
    )(blk_exp, n_used, xs, wg, wu, wd)


def _combine_kernel(x1_ref, rt_ref, ada_ref, g2_ref, b2_ref, yg_ref, o_ref):
    g_f = ada_ref[0][5:6, :]
    rt = rt_ref[...]
    w0 = rt[:, 4:5]
    w1 = rt[:, 5:6]
    y0_lo, y0_hi = _unpack_rows(jnp.concatenate([yg_ref[0, c] for c in range(PLANES)], axis=1))
    y1_lo, y1_hi = _unpack_rows(jnp.concatenate([yg_ref[1, c] for c in range(PLANES)], axis=1))
    f = jnp.concatenate([y0_lo * w0 + y1_lo * w1, y0_hi * w0 + y1_hi * w1], axis=1)
    z = DEEPNORM_ALPHA * x1_ref[...] + g_f * f
    o_ref[...] = _layernorm(z) * g2_ref[...] + b2_ref[...]


def _combine(x1, rt, ada3, g2, b2, yg, seq):
    t = x1.shape[0]
    tm = TM_COMBINE
    tiles_per_seq = seq // tm
    const = lambda i: (0, 0)
    return pl.pallas_call(
        _combine_kernel,
        out_shape=jax.ShapeDtypeStruct((t, D_MODEL), F32),
        grid=(t // tm,),
        in_specs=[
            pl.BlockSpec((tm, D_MODEL), lambda i: (i, 0)),
            pl.BlockSpec((tm, ROUTER_LANES), lambda i: (i, 0)),
            pl.BlockSpec((1, 6, D_MODEL), lambda i: (i // tiles_per_seq, 0, 0)),
            pl.BlockSpec((1, D_MODEL), const),
            pl.BlockSpec((1, D_MODEL), const),
            pl.BlockSpec((2, PLANES, tm, LANES), lambda i: (0, 0, i, 0)),
        ],
        out_specs=pl.BlockSpec((tm, D_MODEL), lambda i: (i, 0)),
        compiler_params=_cparams(("arbitrary",)),
        name="combine",
    )(x1, rt, ada3, g2, b2, yg)


def _rope_tables(seq):
    half = QK_HEAD_DIM // 2
    inv = 1.0 / (ROPE_THETA ** (jnp.arange(0, QK_HEAD_DIM, 2, dtype=F32) / QK_HEAD_DIM))
    ang = jnp.arange(seq, dtype=F32)[:, None] * inv[None, :]
    cos, sin = jnp.cos(ang), jnp.sin(ang)
    cos_t = jnp.tile(cos, (1, LANES // half))
    sin_t = jnp.tile(jnp.concatenate([-sin, sin], axis=1), (1, LANES // QK_HEAD_DIM))
    return cos_t, sin_t


def _routing_tables(rt_t, counts, n_rows_max):
    t = rt_t.shape[1]
    blk = EXPERT_BLOCK
    counts = counts[0, EXPERT_LANE0:EXPERT_LANE0 + N_EXPERTS].astype(jnp.int32)
    padded = ((counts + blk - 1) // blk) * blk
    pend = jnp.cumsum(padded)
    pstart = pend - padded
    e = rt_t[0:2].astype(jnp.int32)
    rank = rt_t[2:4].astype(jnp.int32)
    first_row = jnp.zeros_like(e)
    for j in range(N_EXPERTS):
        first_row = jnp.where(e == j, pstart[j], first_row)
    dest = first_row + rank
    plane0 = (jnp.arange(PLANES, dtype=jnp.int32) * n_rows_max)[None, :, None]
    row_idx = (dest[:, None, :] + plane0).reshape(2 * PLANES * t)
    blk_start = jnp.arange(n_rows_max // blk, dtype=jnp.int32) * blk
    blk_exp = jnp.minimum(jnp.sum(blk_start[:, None] >= pend[None, :], axis=1), N_EXPERTS - 1).astype(jnp.int32)
    n_used = (pend[-1:] // blk).astype(jnp.int32)
    return row_idx, blk_exp, n_used


def _encoder(x, ada, p):
    b, seq, d = x.shape
    t = b * seq
    x2d = x.reshape(t, d)
    ada3 = ada.reshape(b, 6, d)
    cos_t, sin_t = _rope_tables(seq)
    q, k, v, cb, u = _in_proj(x2d, ada3, p["w_in"], cos_t, sin_t, seq)
    attn = _diff_attn(q.reshape(b, seq, 512), k.reshape(b, seq, 512), v.reshape(b, seq, 512),
                      p["lq1"], p["lk1"], p["lq2"], p["lk2"], p["subln_g"])
    x1, h2, rt, rt_t, counts = _mix_route(x2d, attn.reshape(t, 512), cb, u, ada3, p["conv_w"], p["conv_b"],
                                    p["woa"], p["woc"], p["ln1_g"], p["ln1_b"], p["wr"], p["br"], seq)
    n_rows_max = (2 * t // EXPERT_BLOCK + N_EXPERTS) * EXPERT_BLOCK
    row_idx, blk_exp, n_used = _routing_tables(rt_t, counts, n_rows_max)
    xs = _sc_scatter_rows(h2.reshape(PLANES * t, LANES), row_idx, PLANES * n_rows_max)
    y = _experts(blk_exp, n_used, xs.reshape(PLANES, n_rows_max, LANES), p["wg"], p["wu"], p["wd"])
    yg = _sc_gather_rows(y.reshape(PLANES * n_rows_max, LANES), row_idx)
    out = _combine(x1, rt, ada3, p["ln2_g"], p["ln2_b"], yg.reshape(2, PLANES, t, LANES), seq)
    return out.reshape(b, seq, d)


def kernel(x_prompt, x_sample, c_prompt, c_sample, w_ada, b_ada, w_in, lambda_q1, lambda_k1, lambda_q2, lambda_k2, attn_subln_g, conv_w, conv_b, w_out, ln1_g, ln1_b, router_group_w, router_group_b, router_expert_w, router_expert_b, expert_w_gate, expert_w_up, expert_w_down, ln2_g, ln2_b):
    l = 0
    wr = jnp.concatenate([router_group_w[l], router_expert_w[l]], axis=1)
    wr = jnp.pad(wr, ((0, 0), (0, ROUTER_LANES - wr.shape[1])))
    wr_hi, wr_lo = _split_hi_lo(wr)
    br = jnp.concatenate([router_group_b[l], router_expert_b[l]])
    br = jnp.pad(br, (0, ROUTER_LANES - br.shape[0])).reshape(1, ROUTER_LANES)
    w_out_bf = w_out[l].astype(BF16)
    p = {
        "w_in": w_in[l].astype(BF16),
        "lq1": lambda_q1[l].reshape(1, -1), "lk1": lambda_k1[l].reshape(1, -1),
        "lq2": lambda_q2[l].reshape(1, -1), "lk2": lambda_k2[l].reshape(1, -1),
        "subln_g": attn_subln_g[l].reshape(1, -1),
        "conv_w": conv_w[l], "conv_b": conv_b[l].reshape(1, -1),
        "woa": w_out_bf[:ATTN_WIDTH], "woc": w_out_bf[ATTN_WIDTH:],
        "ln1_g": ln1_g[l].reshape(1, -1), "ln1_b": ln1_b[l].reshape(1, -1),
        "wr": jnp.concatenate([wr_hi, wr_lo], axis=1), "br": br,
        "wg": expert_w_gate[l], "wu": expert_w_up[l], "wd": expert_w_down[l],
        "ln2_g": ln2_g[l].reshape(1, -1), "ln2_b": ln2_b[l].reshape(1, -1),
    }
    ada = _ada(jnp.concatenate([c_prompt, c_sample], axis=0), w_ada[l], b_ada[l].reshape(1, -1))
    n_prompt = c_prompt.shape[0]
    y_prompt = _encoder(x_prompt, ada[:n_prompt], p)
    y_sample = _encoder(x_sample, ada[n_prompt:], p)
    return (y_prompt, y_sample)
```

```python
import functools
import math

import jax
import jax.numpy as jnp
from jax import lax
from jax.experimental import pallas as pl
from jax.experimental.pallas import tpu as pltpu
from jax.experimental.pallas import tpu_sc as plsc

D_MODEL = 1024
ATTN_WIDTH = 512
CONV_WIDTH = 512
QK_HEAD_DIM = 64
V_HEAD_DIM = 128
N_DIFF_HEADS = 4
Q_WIDTH = 512
IN_WIDTH = 3072
COL_Q, COL_K, COL_V, COL_B, COL_C, COL_H, COL_END = (0, Q_WIDTH, 2 * Q_WIDTH, 2 * Q_WIDTH + ATTN_WIDTH,
                                                       2 * Q_WIDTH + ATTN_WIDTH + CONV_WIDTH,
                                                       2 * Q_WIDTH + ATTN_WIDTH + 2 * CONV_WIDTH, IN_WIDTH)
ADA_SH_M, ADA_SC_M, ADA_G_M, ADA_SH_F, ADA_SC_F, ADA_G_F = range(6)
CONV_K = 3
ROPE_THETA = 10000.0
N_GROUPS = 4
EXPERTS_PER_GROUP = 8
N_EXPERTS = 32
D_FF_EXPERT = 512
LN_EPS = 1e-5
RMS_EPS = 1e-5
DEPTH = 1
DEEPNORM_ALPHA = (2.0 * DEPTH) ** 0.25
LAMBDA_INIT = 0.8 - 0.6 * math.exp(-0.3 * 0)

LANES = 128
BF16_SUBLANES = 16
VMEM_LIMIT = 48 * 1024 * 1024

ADA_COLS = 1024
TM_PROJ = 1024
TQ_MAX = 1024
Q_TILES_MIN = 4
TK = 512
TM_MIX = 1024
TM_COMBINE = 1024
SC_WINDOW = 256
EXPERT_BLOCK = 1024
ROUTER_LANES = 128
EXPERT_LANE0 = N_GROUPS
ROUTE_FIELDS = 8

BF16 = jnp.bfloat16
F32 = jnp.float32


def _cparams(sem):
    return pltpu.CompilerParams(dimension_semantics=sem, vmem_limit_bytes=VMEM_LIMIT)


def _layernorm(x):
    mu = jnp.mean(x, axis=-1, keepdims=True)
    xc = x - mu
    var = jnp.mean(xc * xc, axis=-1, keepdims=True)
    return xc * lax.rsqrt(var + LN_EPS)


def _split_hi_lo(a):
    hi = a.astype(BF16)
    lo = (a - hi.astype(F32)).astype(BF16)
    return hi, lo


def _dot(a, b):
    return jnp.dot(a, b, preferred_element_type=F32)


HALF = D_MODEL // 2
U32 = jnp.uint32
PLANES = HALF // LANES


def _pack_rows(x):
    lo = lax.bitcast_convert_type(x[:, :HALF].astype(BF16).astype(F32), U32)
    hi = lax.bitcast_convert_type(x[:, HALF:].astype(BF16).astype(F32), U32)
    return (lo >> 16) | hi


def _unpack_rows(w):
    lo = lax.bitcast_convert_type(w << 16, F32)
    hi = lax.bitcast_convert_type(w & jnp.uint32(0xFFFF0000), F32)
    return lo, hi


def _ada_kernel(c_ref, w_ref, b_ref, o_ref):
    c = c_ref[...]
    s = c * jax.nn.sigmoid(c)
    s_hi, s_lo = _split_hi_lo(s)
    w_hi, w_lo = _split_hi_lo(w_ref[...])
    acc = _dot(s_hi, w_hi) + _dot(s_lo, w_hi) + _dot(s_hi, w_lo)
    o_ref[...] = acc + b_ref[...]


def _ada(c, w, b):
    bsz = c.shape[0]
    n = w.shape[1]
    tn = ADA_COLS
    return pl.pallas_call(
        _ada_kernel,
        out_shape=jax.ShapeDtypeStruct((bsz, n), F32),
        grid=(n // tn,),
        in_specs=[
            pl.BlockSpec((bsz, D_MODEL), lambda j: (0, 0)),
            pl.BlockSpec((D_MODEL, tn), lambda j: (0, j)),
            pl.BlockSpec((1, tn), lambda j: (0, j)),
        ],
        out_specs=pl.BlockSpec((bsz, tn), lambda j: (0, j)),
        compiler_params=_cparams(("arbitrary",)),
        name="ada",
    )(c, w, b)


def _rope(x, cos_t, sin_t):
    half = QK_HEAD_DIM // 2
    lane = lax.broadcasted_iota(jnp.int32, x.shape, 1)
    upper = (lane & half) != 0
    partner = jnp.where(upper, pltpu.roll(x, half, axis=1), pltpu.roll(x, LANES - half, axis=1))
    return x * cos_t + partner * sin_t


def _in_proj_kernel(x_ref, ada_ref, w_ref, cos_ref, sin_ref,
                    q_ref, k_ref, v_ref, cb_ref, u_ref):
    ada = ada_ref[0]
    sh_m = ada[ADA_SH_M:ADA_SH_M + 1, :]
    sc_m = ada[ADA_SC_M:ADA_SC_M + 1, :]
    qk_scale = QK_HEAD_DIM ** -0.5 * math.log2(math.e)
    hm = TM_PROJ // 2
    halves = [slice(h * hm, (h + 1) * hm) for h in range(2)]
    hs = [(_layernorm(x_ref[rows, :]) * (1.0 + sc_m) + sh_m).astype(BF16) for rows in halves]
    for rows, h in zip(halves, hs):
        cos_t = cos_ref[rows, :]
        sin_t = sin_ref[rows, :]
        q = _dot(h, w_ref[:, COL_Q:COL_K])
        k = _dot(h, w_ref[:, COL_K:COL_V])
        for j in range(Q_WIDTH // LANES):
            lo = j * LANES
            q_ref[rows, lo:lo + LANES] = (_rope(q[:, lo:lo + LANES], cos_t, sin_t) * qk_scale).astype(BF16)
            k_ref[rows, lo:lo + LANES] = _rope(k[:, lo:lo + LANES], cos_t, sin_t).astype(BF16)
        v_ref[rows, :] = _dot(h, w_ref[:, COL_V:COL_B]).astype(BF16)
        cb_ref[rows, :] = _dot(h, w_ref[:, COL_B:COL_C]).astype(BF16)
        cc = _dot(h, w_ref[:, COL_C:COL_H])
        ch = _dot(h, w_ref[:, COL_H:COL_END])
        u_ref[rows, :] = (cc * ch).astype(BF16)


def _in_proj(x2d, ada3, w_in_bf, cos_t, sin_t, seq):
    t = x2d.shape[0]
    tm = TM_PROJ
    tiles_per_seq = seq // tm
    out = jax.ShapeDtypeStruct((t, Q_WIDTH), BF16)
    ospec = pl.BlockSpec((tm, Q_WIDTH), lambda i: (i, 0))
    return pl.pallas_call(
        _in_proj_kernel,
        out_shape=(out,) * 5,
        grid=(t // tm,),
        in_specs=[
            pl.BlockSpec((tm, D_MODEL), lambda i: (i, 0)),
            pl.BlockSpec((1, 6, D_MODEL), lambda i: (i // tiles_per_seq, 0, 0)),
            pl.BlockSpec((D_MODEL, IN_WIDTH), lambda i: (0, 0)),
            pl.BlockSpec((tm, LANES), lambda i: (i % tiles_per_seq, 0)),
            pl.BlockSpec((tm, LANES), lambda i: (i % tiles_per_seq, 0)),
        ],
        out_specs=(ospec,) * 5,
        compiler_params=_cparams(("arbitrary",)),
        name="in_proj",
    )(x2d, ada3, w_in_bf, cos_t, sin_t)


def _diff_attn_kernel(q_ref, k_ref, v_ref, lq1_ref, lk1_ref, lq2_ref, lk2_ref, g_ref, o_ref,
                      qq_ref, vx_ref, s_buf, m_ref, acc_ref, *, seq, tq, tk):
    nq = seq // tq
    nk = seq // tk
    n_lane_blocks = tk // LANES

    vx_ref[:, 0:V_HEAD_DIM] = v_ref[0]
    vx_ref[:, V_HEAD_DIM:2 * V_HEAD_DIM] = jnp.ones((seq, V_HEAD_DIM), BF16)
    lam = (jnp.exp(jnp.sum(lq1_ref[...] * lk1_ref[...], axis=-1, keepdims=True))
           - jnp.exp(jnp.sum(lq2_ref[...] * lk2_ref[...], axis=-1, keepdims=True)) + LAMBDA_INIT)

    def load_q(qt, qslot):
        q = q_ref[0, pl.ds(pl.multiple_of(qt * tq, tq), tq), :]
        lane = lax.broadcasted_iota(jnp.int32, q.shape, 1)
        zero = jnp.zeros_like(q)
        qq_ref[qslot, 0:tq, :] = jnp.where(lane < QK_HEAD_DIM, q, zero)
        qq_ref[qslot, tq:2 * tq, :] = jnp.where(lane >= QK_HEAD_DIM, q, zero)

    def scores(qslot, j, slot):
        start = pl.multiple_of(j * tk, tk)
        kc = k_ref[0, pl.ds(start, tk), :]
        s_buf[slot] = lax.dot_general(qq_ref[qslot], kc, (((1,), (1,)), ((), ())), preferred_element_type=F32)

    def softmax_pv(j, slot):
        blocks = [s_buf[slot, :, c * LANES:(c + 1) * LANES] for c in range(n_lane_blocks)]
        mb = blocks[0]
        for c in range(1, n_lane_blocks):
            mb = jnp.maximum(mb, blocks[c])
        m_old = m_ref[...]
        m_new = jnp.maximum(m_old, jnp.max(mb, axis=-1, keepdims=True))
        alpha = jnp.exp2(m_old - m_new)
        m_ref[...] = m_new
        p = jnp.concatenate([jnp.exp2(blk - m_new).astype(BF16) for blk in blocks], axis=1)
        start = pl.multiple_of(j * tk, tk)
        pv = _dot(p, vx_ref[pl.ds(start, tk), :])
        acc_ref[:, 0:V_HEAD_DIM] = alpha * acc_ref[:, 0:V_HEAD_DIM] + pv[:, 0:V_HEAD_DIM]
        acc_ref[:, V_HEAD_DIM:] = alpha * acc_ref[:, V_HEAD_DIM:] + pv[:, V_HEAD_DIM:]

    def q_tile(qt, qslot, has_next):
        m_ref[...] = jnp.full(m_ref.shape, -jnp.inf, F32)
        acc_ref[...] = jnp.zeros(acc_ref.shape, F32)

        def pair(jj, c):
            j = 2 * jj
            scores(qslot, j + 1, 1)
            softmax_pv(j, 0)
            scores(qslot, j + 2, 0)
            softmax_pv(j + 1, 1)
            return c

        lax.fori_loop(0, nk // 2 - 1, pair, 0)
        scores(qslot, nk - 1, 1)
        softmax_pv(nk - 2, 0)
        if has_next:
            load_q(qt + 1, 1 - qslot)
            scores(1 - qslot, 0, 0)
        softmax_pv(nk - 1, 1)

        o = acc_ref[:, 0:V_HEAD_DIM] / acc_ref[:, V_HEAD_DIM:]
        of = o[:tq] - lam * o[tq:]
        of = of * lax.rsqrt(jnp.mean(of * of, axis=-1, keepdims=True) + RMS_EPS)
        of = of * g_ref[...] * (1.0 - LAMBDA_INIT)
        o_ref[0, pl.ds(pl.multiple_of(qt * tq, tq), tq), :] = of.astype(BF16)

    load_q(0, 0)
    scores(0, 0, 0)

    def q_pair(i, c):
        q_tile(2 * i, 0, True)
        q_tile(2 * i + 1, 1, True)
        return c

    lax.fori_loop(0, nq // 2 - 1, q_pair, 0)
    q_tile(nq - 2, 0, True)
    q_tile(nq - 1, 1, False)


def _diff_attn(q, k, v, lq1, lk1, lq2, lk2, g):
    b, seq, _ = q.shape
    tk = TK
    tq = min(TQ_MAX, seq // Q_TILES_MIN)
    lam_spec = pl.BlockSpec((1, QK_HEAD_DIM), lambda bi, h: (0, 0))
    head_spec = pl.BlockSpec((1, seq, LANES), lambda bi, h: (bi, 0, h))
    return pl.pallas_call(
        functools.partial(_diff_attn_kernel, seq=seq, tq=tq, tk=tk),
        out_shape=jax.ShapeDtypeStruct((b, seq, ATTN_WIDTH), BF16),
        grid=(b, N_DIFF_HEADS),
        in_specs=[
            head_spec, head_spec, head_spec,
            lam_spec, lam_spec, lam_spec, lam_spec,
            pl.BlockSpec((1, V_HEAD_DIM), lambda bi, h: (0, 0)),
        ],
        out_specs=head_spec,
        scratch_shapes=[
            pltpu.VMEM((2, 2 * tq, LANES), BF16),
            pltpu.VMEM((seq, 2 * V_HEAD_DIM), BF16),
            pltpu.VMEM((2, 2 * tq, tk), F32),
            pltpu.VMEM((2 * tq, LANES), F32),
            pltpu.VMEM((2 * tq, 2 * V_HEAD_DIM), F32),
        ],
        compiler_params=_cparams(("arbitrary", "arbitrary")),
        name="diff_attn",
    )(q, k, v, lq1, lk1, lq2, lk2, g)


def _lane_min_index(mask, lane_f):
    return jnp.min(jnp.where(mask, lane_f, float(ROUTER_LANES)), axis=-1, keepdims=True)


def _mix_route_kernel(x_ref, attn_ref, cb_ref, u_ref, uprev_ref, unext_ref, ada_ref,
                      cw_ref, cbias_ref, woa_ref, woc_ref, g1_ref, b1_ref, wr_ref, br_ref,
                      x1_ref, h2_ref, rt_ref, rtt_ref, cnt_ref, base_ref, *, seq):
    i = pl.program_id(0)
    tm = TM_MIX

    @pl.when(i == 0)
    def _():
        base_ref[...] = jnp.zeros_like(base_ref)

    ada = ada_ref[0]
    g_m = ada[ADA_G_M:ADA_G_M + 1, :]
    sh_f = ada[ADA_SH_F:ADA_SH_F + 1, :]
    sc_f = ada[ADA_SC_F:ADA_SC_F + 1, :]

    u = u_ref[...].astype(F32)
    row = lax.broadcasted_iota(jnp.int32, u.shape, 0)
    not_seq_start = ((i * tm) % seq != 0).astype(F32)
    not_seq_end = (((i + 1) * tm) % seq != 0).astype(F32)
    halo_prev = uprev_ref[...].astype(F32)[BF16_SUBLANES - 1:BF16_SUBLANES, :] * not_seq_start
    halo_next = unext_ref[...].astype(F32)[0:1, :] * not_seq_end
    u_prev = jnp.where(row == 0, halo_prev, pltpu.roll(u, 1, axis=0))
    u_next = jnp.where(row == tm - 1, halo_next, pltpu.roll(u, tm - 1, axis=0))
    cw = cw_ref[...]

    hm = tm // 2
    halves = [slice(h * hm, (h + 1) * hm) for h in range(2)]
    mixes = []
    for rows in halves:
        y = cbias_ref[...] + u_prev[rows, :] * cw[0:1, :]
        y = y + u[rows, :] * cw[1:2, :]
        y = y + u_next[rows, :] * cw[2:3, :]
        conv = (cb_ref[rows, :].astype(F32) * y).astype(BF16)
        mixes.append(_dot(attn_ref[rows, :], woa_ref[...]) + _dot(conv, woc_ref[...]))

    lane = lax.broadcasted_iota(jnp.int32, (hm, ROUTER_LANES), 1)
    lane_f = lane.astype(F32)
    neg = -jnp.inf
    tri = jnp.where(lax.broadcasted_iota(jnp.int32, (hm, hm), 1) < lax.broadcasted_iota(jnp.int32, (hm, hm), 0),
                    1.0, 0.0).astype(BF16)
    base = base_ref[...]
    for rows, mix in zip(halves, mixes):
        x1 = _layernorm(DEEPNORM_ALPHA * x_ref[rows, :] + g_m * mix) * g1_ref[...] + b1_ref[...]
        x1_ref[rows, :] = x1
        h2 = _layernorm(x1) * (1.0 + sc_f) + sh_f
        for c, plane in enumerate(_to_planes(_pack_rows(h2))):
            h2_ref[c, rows, :] = plane

        h_hi, h_lo = _split_hi_lo(h2)
        both = _dot(h_hi, wr_ref[...])
        logits = (both[:, :ROUTER_LANES] + both[:, ROUTER_LANES:]
                  + _dot(h_lo, wr_ref[:, :ROUTER_LANES]) + br_ref[...])

        lg = jnp.where(lane < N_GROUPS, logits, neg)
        lg_max = jnp.max(lg, axis=-1, keepdims=True)
        g_sel = _lane_min_index(lg == lg_max, lane_f)
        pg_sel = 1.0 / jnp.sum(jnp.exp(lg - lg_max), axis=-1, keepdims=True)

        first = EXPERT_LANE0 + EXPERTS_PER_GROUP * g_sel
        in_group = (lane_f >= first) & (lane_f < first + EXPERTS_PER_GROUP)
        le = jnp.where(in_group, logits, neg)
        l0 = jnp.max(le, axis=-1, keepdims=True)
        i0 = _lane_min_index(le == l0, lane_f)
        le2 = jnp.where(lane_f == i0, neg, le)
        l1 = jnp.max(le2, axis=-1, keepdims=True)
        i1 = _lane_min_index(le2 == l1, lane_f)
        t_exp = jnp.exp(l1 - l0)
        p0 = 1.0 / (1.0 + t_exp)
        w0 = pg_sel * p0
        w1 = pg_sel * (t_exp * p0)

        oh0 = lane_f == i0
        oh1 = lane_f == i1
        onehots = jnp.concatenate([jnp.where(oh0, 1.0, 0.0), jnp.where(oh1, 1.0, 0.0)], axis=1).astype(BF16)
        before = _dot(tri, onehots)
        cnt0 = jnp.sum(jnp.where(oh0, 1.0, 0.0), axis=0, keepdims=True)
        cnt1 = jnp.sum(jnp.where(oh1, 1.0, 0.0), axis=0, keepdims=True)
        rank0 = jnp.sum(jnp.where(oh0, before[:, :ROUTER_LANES] + base, 0.0), axis=-1, keepdims=True)
        rank1 = jnp.sum(jnp.where(oh1, before[:, ROUTER_LANES:] + base + cnt0, 0.0), axis=-1, keepdims=True)
        base = base + cnt0 + cnt1

        rt = jnp.where(lane == 0, i0 - EXPERT_LANE0, 0.0)
        rt = jnp.where(lane == 1, i1 - EXPERT_LANE0, rt)
        rt = jnp.where(lane == 2, rank0, rt)
        rt = jnp.where(lane == 3, rank1, rt)
        rt = jnp.where(lane == 4, w0, rt)
        rt = jnp.where(lane == 5, w1, rt)
        rt_ref[rows, :] = rt
        rtt_ref[:, rows] = rt.T[0:ROUTE_FIELDS, :]
    base_ref[...] = base
    cnt_ref[...] = base


def _mix_route(x2d, attn2d, cb, u, ada3, conv_w, conv_b, woa, woc, g1, b1, wr, br, seq):
    t = x2d.shape[0]
    tm = TM_MIX
    tiles_per_seq = seq // tm
    hb = tm // BF16_SUBLANES
    n_halo = t // BF16_SUBLANES
    const = lambda i: (0, 0)
    return pl.pallas_call(
        functools.partial(_mix_route_kernel, seq=seq),
        out_shape=(
            jax.ShapeDtypeStruct((t, D_MODEL), F32),
            jax.ShapeDtypeStruct((PLANES, t, LANES), U32),
            jax.ShapeDtypeStruct((t, ROUTER_LANES), F32),
            jax.ShapeDtypeStruct((ROUTE_FIELDS, t), F32),
            jax.ShapeDtypeStruct((1, ROUTER_LANES), F32),
        ),
        grid=(t // tm,),
        in_specs=[
            pl.BlockSpec((tm, D_MODEL), lambda i: (i, 0)),
            pl.BlockSpec((tm, ATTN_WIDTH), lambda i: (i, 0)),
            pl.BlockSpec((tm, CONV_WIDTH), lambda i: (i, 0)),
            pl.BlockSpec((tm, CONV_WIDTH), lambda i: (i, 0)),
            pl.BlockSpec((BF16_SUBLANES, CONV_WIDTH), lambda i: (jnp.maximum(i * hb - 1, 0), 0)),
            pl.BlockSpec((BF16_SUBLANES, CONV_WIDTH), lambda i: (jnp.minimum((i + 1) * hb, n_halo - 1), 0)),
            pl.BlockSpec((1, 6, D_MODEL), lambda i: (i // tiles_per_seq, 0, 0)),
            pl.BlockSpec((CONV_K, CONV_WIDTH), const),
            pl.BlockSpec((1, CONV_WIDTH), const),
            pl.BlockSpec((ATTN_WIDTH, D_MODEL), const),
            pl.BlockSpec((CONV_WIDTH, D_MODEL), const),
            pl.BlockSpec((1, D_MODEL), const),
            pl.BlockSpec((1, D_MODEL), const),
            pl.BlockSpec((D_MODEL, 2 * ROUTER_LANES), const),
            pl.BlockSpec((1, ROUTER_LANES), const),
        ],
        out_specs=(
            pl.BlockSpec((tm, D_MODEL), lambda i: (i, 0)),
            pl.BlockSpec((PLANES, tm, LANES), lambda i: (0, i, 0)),
            pl.BlockSpec((tm, ROUTER_LANES), lambda i: (i, 0)),
            pl.BlockSpec((ROUTE_FIELDS, tm), lambda i: (0, i)),
            pl.BlockSpec((1, ROUTER_LANES), const),
        ),
        scratch_shapes=[pltpu.VMEM((1, ROUTER_LANES), F32)],
        compiler_params=_cparams(("arbitrary",)),
        name="mix_route",
    )(x2d, attn2d, cb, u, u, u, ada3, conv_w, conv_b, woa, woc, g1, b1, wr, br)


def _sc_mesh():
    return plsc.VectorSubcoreMesh(core_axis_name="core", subcore_axis_name="subcore")


def _sc_scatter_rows(src, idx, n_dst):
    m = idx.shape[0]
    w = SC_WINDOW
    n_src_blocks = src.shape[0] // w

    @pl.kernel(out_type=jax.ShapeDtypeStruct((n_dst, LANES), src.dtype), mesh=_sc_mesh(), scratch_types=[])
    def scatter(x_hbm, i_hbm, o_hbm):
        def body(x_vmem, i_vmem):
            pltpu.sync_copy(x_vmem, o_hbm.at[i_vmem.at[0]])

        pltpu.emit_pipeline(
            body, grid=(m // w,),
            in_specs=[pl.BlockSpec((w, LANES), index_map=lambda i: (i % n_src_blocks, 0)),
                      pl.BlockSpec((1, w), index_map=lambda i: (0, i))],
            out_specs=[],
            core_axis_name=("core", "subcore"),
            dimension_semantics=(pltpu.PARALLEL,),
        )(x_hbm, i_hbm)

    return scatter(src, idx.reshape(1, m))


def _sc_gather_rows(src, idx):
    m = idx.shape[0]
    w = SC_WINDOW

    @pl.kernel(out_type=jax.ShapeDtypeStruct((m, LANES), src.dtype), mesh=_sc_mesh(), scratch_types=[])
    def gather(x_hbm, i_hbm, o_hbm):
        def body(i_vmem, o_vmem):
            pltpu.sync_copy(x_hbm.at[i_vmem.at[0]], o_vmem)

        pltpu.emit_pipeline(
            body, grid=(m // w,),
            in_specs=[pl.BlockSpec((1, w), index_map=lambda i: (0, i))],
            out_specs=[pl.BlockSpec((w, LANES), index_map=lambda i: (i, 0))],
            core_axis_name=("core", "subcore"),
            dimension_semantics=(pltpu.PARALLEL,),
        )(i_hbm, o_hbm)

    return gather(src, idx.reshape(1, m))


def _to_planes(words):
    return [words[:, c * LANES:(c + 1) * LANES] for c in range(PLANES)]


def _experts_kernel(blk_exp_ref, n_used_ref, xs_ref, wg_ref, wu_ref, wd_ref, y_ref):
    del blk_exp_ref

    @pl.when(pl.program_id(0) < n_used_ref[0])
    def _():
        x_lo, x_hi = _unpack_rows(jnp.concatenate([xs_ref[c] for c in range(PLANES)], axis=1))
        x_lo = x_lo.astype(BF16)
        x_hi = x_hi.astype(BF16)
        g = (_dot(x_lo, wg_ref[0, :HALF, :].astype(BF16))
             + _dot(x_hi, wg_ref[0, HALF:, :].astype(BF16)))
        up = (_dot(x_lo, wu_ref[0, :HALF, :].astype(BF16))
              + _dot(x_hi, wu_ref[0, HALF:, :].astype(BF16)))
        act = (g * jax.nn.sigmoid(g) * up).astype(BF16)
        for c, plane in enumerate(_to_planes(_pack_rows(_dot(act, wd_ref[0].astype(BF16))))):
            y_ref[c] = plane


def _experts(blk_exp, n_used, xs, wg, wu, wd):
    rows = xs.shape[1]
    blk = EXPERT_BLOCK

    def last_used(i, nu):
        return jnp.minimum(i, nu[0] - 1)

    return pl.pallas_call(
        _experts_kernel,
        out_shape=jax.ShapeDtypeStruct((PLANES, rows, LANES), U32),
        grid_spec=pltpu.PrefetchScalarGridSpec(
            num_scalar_prefetch=2,
            grid=(rows // blk,),
            in_specs=[
                pl.BlockSpec((PLANES, blk, LANES), lambda i, be, nu: (0, last_used(i, nu), 0)),
                pl.BlockSpec((1, D_MODEL, D_FF_EXPERT), lambda i, be, nu: (be[last_used(i, nu)], 0, 0)),
                pl.BlockSpec((1, D_MODEL, D_FF_EXPERT), lambda i, be, nu: (be[last_used(i, nu)], 0, 0)),
                pl.BlockSpec((1, D_FF_EXPERT, D_MODEL), lambda i, be, nu: (be[last_used(i, nu)], 0, 0)),
            ],
            out_specs=pl.BlockSpec((PLANES, blk, LANES), lambda i, be, nu: (0, last_used(i, nu), 0)),
        ),
        compiler_params=_cparams(("arbitrary",)),
        name="experts",
    )(blk_exp, n_used, xs, wg, wu, wd)


def _combine_kernel(x1_ref, rt_ref, ada_ref, g2_ref, b2_ref, yg_ref, o_ref):
    g_f = ada_ref[0][ADA_G_F:ADA_G_F + 1, :]
    rt = rt_ref[...]
    w0 = rt[:, 4:5]
    w1 = rt[:, 5:6]
    y0_lo, y0_hi = _unpack_rows(jnp.concatenate([yg_ref[0, c] for c in range(PLANES)], axis=1))
    y1_lo, y1_hi = _unpack_rows(jnp.concatenate([yg_ref[1, c] for c in range(PLANES)], axis=1))
    f = jnp.concatenate([y0_lo * w0 + y1_lo * w1, y0_hi * w0 + y1_hi * w1], axis=1)
    z = DEEPNORM_ALPHA * x1_ref[...] + g_f * f
    o_ref[...] = _layernorm(z) * g2_ref[...] + b2_ref[...]


def _combine(x1, rt, ada3, g2, b2, yg, seq):
    t = x1.shape[0]
    tm = TM_COMBINE
    tiles_per_seq = seq // tm
    const = lambda i: (0, 0)
    return pl.pallas_call(
        _combine_kernel,
        out_shape=jax.ShapeDtypeStruct((t, D_MODEL), F32),
        grid=(t // tm,),
        in_specs=[
            pl.BlockSpec((tm, D_MODEL), lambda i: (i, 0)),
            pl.BlockSpec((tm, ROUTER_LANES), lambda i: (i, 0)),
            pl.BlockSpec((1, 6, D_MODEL), lambda i: (i // tiles_per_seq, 0, 0)),
            pl.BlockSpec((1, D_MODEL), const),
            pl.BlockSpec((1, D_MODEL), const),
            pl.BlockSpec((2, PLANES, tm, LANES), lambda i: (0, 0, i, 0)),
        ],
        out_specs=pl.BlockSpec((tm, D_MODEL), lambda i: (i, 0)),
        compiler_params=_cparams(("arbitrary",)),
        name="combine",
    )(x1, rt, ada3, g2, b2, yg)


def _rope_tables(seq):
    half = QK_HEAD_DIM // 2
    inv = 1.0 / (ROPE_THETA ** (jnp.arange(0, QK_HEAD_DIM, 2, dtype=F32) / QK_HEAD_DIM))
    ang = jnp.arange(seq, dtype=F32)[:, None] * inv[None, :]
    cos, sin = jnp.cos(ang), jnp.sin(ang)
    cos_t = jnp.tile(cos, (1, LANES // half))
    sin_t = jnp.tile(jnp.concatenate([-sin, sin], axis=1), (1, LANES // QK_HEAD_DIM))
    return cos_t, sin_t


def _routing_tables(rt_t, counts, n_rows_max):
    t = rt_t.shape[1]
    blk = EXPERT_BLOCK
    counts = counts[0, EXPERT_LANE0:EXPERT_LANE0 + N_EXPERTS].astype(jnp.int32)
    padded = ((counts + blk - 1) // blk) * blk
    pend = jnp.cumsum(padded)
    pstart = pend - padded
    e = rt_t[0:2].astype(jnp.int32)
    rank = rt_t[2:4].astype(jnp.int32)
    first_row = jnp.zeros_like(e)
    for j in range(N_EXPERTS):
        first_row = jnp.where(e == j, pstart[j], first_row)
    dest = first_row + rank
    plane0 = (jnp.arange(PLANES, dtype=jnp.int32) * n_rows_max)[None, :, None]
    row_idx = (dest[:, None, :] + plane0).reshape(2 * PLANES * t)
    blk_start = jnp.arange(n_rows_max // blk, dtype=jnp.int32) * blk
    blk_exp = jnp.minimum(jnp.sum(blk_start[:, None] >= pend[None, :], axis=1), N_EXPERTS - 1).astype(jnp.int32)
    n_used = (pend[-1:] // blk).astype(jnp.int32)
    return row_idx, blk_exp, n_used


def _encoder(x, ada, p):
    b, seq, d = x.shape
    t = b * seq
    x2d = x.reshape(t, d)
    ada3 = ada.reshape(b, 6, d)
    cos_t, sin_t = _rope_tables(seq)
    q, k, v, cb, u = _in_proj(x2d, ada3, p["w_in"], cos_t, sin_t, seq)
    attn = _diff_attn(q.reshape(b, seq, Q_WIDTH), k.reshape(b, seq, Q_WIDTH), v.reshape(b, seq, ATTN_WIDTH),
                      p["lq1"], p["lk1"], p["lq2"], p["lk2"], p["subln_g"])
    x1, h2, rt, rt_t, counts = _mix_route(x2d, attn.reshape(t, ATTN_WIDTH), cb, u, ada3, p["conv_w"], p["conv_b"],
                                    p["woa"], p["woc"], p["ln1_g"], p["ln1_b"], p["wr"], p["br"], seq)
    n_rows_max = (2 * t // EXPERT_BLOCK + N_EXPERTS) * EXPERT_BLOCK
    row_idx, blk_exp, n_used = _routing_tables(rt_t, counts, n_rows_max)
    xs = _sc_scatter_rows(h2.reshape(PLANES * t, LANES), row_idx, PLANES * n_rows_max)
    y = _experts(blk_exp, n_used, xs.reshape(PLANES, n_rows_max, LANES), p["wg"], p["wu"], p["wd"])
    yg = _sc_gather_rows(y.reshape(PLANES * n_rows_max, LANES), row_idx)
    out = _combine(x1, rt, ada3, p["ln2_g"], p["ln2_b"], yg.reshape(2, PLANES, t, LANES), seq)
    return out.reshape(b, seq, d)


def kernel(x_prompt, x_sample, c_prompt, c_sample, w_ada, b_ada, w_in, lambda_q1, lambda_k1, lambda_q2, lambda_k2, attn_subln_g, conv_w, conv_b, w_out, ln1_g, ln1_b, router_group_w, router_group_b, router_expert_w, router_expert_b, expert_w_gate, expert_w_up, expert_w_down, ln2_g, ln2_b):
    l = 0
    wr = jnp.concatenate([router_group_w[l], router_expert_w[l]], axis=1)
    wr = jnp.pad(wr, ((0, 0), (0, ROUTER_LANES - wr.shape[1])))
    wr_hi, wr_lo = _split_hi_lo(wr)
    br = jnp.concatenate([router_group_b[l], router_expert_b[l]])
    br = jnp.pad(br, (0, ROUTER_LANES - br.shape[0])).reshape(1, ROUTER_LANES)
    w_out_bf = w_out[l].astype(BF16)
    p = {
        "w_in": w_in[l].astype(BF16),
        "lq1": lambda_q1[l].reshape(1, -1), "lk1": lambda_k1[l].reshape(1, -1),
        "lq2": lambda_q2[l].reshape(1, -1), "lk2": lambda_k2[l].reshape(1, -1),
        "subln_g": attn_subln_g[l].reshape(1, -1),
        "conv_w": conv_w[l], "conv_b": conv_b[l].reshape(1, -1),
        "woa": w_out_bf[:ATTN_WIDTH], "woc": w_out_bf[ATTN_WIDTH:],
        "ln1_g": ln1_g[l].reshape(1, -1), "ln1_b": ln1_b[l].reshape(1, -1),
        "wr": jnp.concatenate([wr_hi, wr_lo], axis=1), "br": br,
        "wg": expert_w_gate[l], "wu": expert_w_up[l], "wd": expert_w_down[l],
        "ln2_g": ln2_g[l].reshape(1, -1), "ln2_b": ln2_b[l].reshape(1, -1),
    }
    ada = _ada(jnp.concatenate([c_prompt, c_sample], axis=0), w_ada[l], b_ada[l].reshape(1, -1))
    n_prompt = c_prompt.shape[0]
    y_prompt = _encoder(x_prompt, ada[:n_prompt], p)
    y_sample = _encoder(x_sample, ada[n_prompt:], p)
    return (y_prompt, y_sample)
```

```python
import functools
import math

import jax
import jax.numpy as jnp
from jax import lax
from jax.experimental import pallas as pl
from jax.experimental.pallas import tpu as pltpu
from jax.experimental.pallas import tpu_sc as plsc

D_MODEL = 1024
ATTN_WIDTH = 512
CONV_WIDTH = 512
QK_HEAD_DIM = 64
V_HEAD_DIM = 128
N_DIFF_HEADS = 4
Q_WIDTH = 512
IN_WIDTH = 3072
COL_Q, COL_K, COL_V, COL_B, COL_C, COL_H, COL_END = (0, Q_WIDTH, 2 * Q_WIDTH, 2 * Q_WIDTH + ATTN_WIDTH,
                                                       2 * Q_WIDTH + ATTN_WIDTH + CONV_WIDTH,
                                                       2 * Q_WIDTH + ATTN_WIDTH + 2 * CONV_WIDTH, IN_WIDTH)
ADA_SH_M, ADA_SC_M, ADA_G_M, ADA_SH_F, ADA_SC_F, ADA_G_F = range(6)
CONV_K = 3
ROPE_THETA = 10000.0
N_GROUPS = 4
EXPERTS_PER_GROUP = 8
N_EXPERTS = 32
D_FF_EXPERT = 512
LN_EPS = 1e-5
RMS_EPS = 1e-5
DEPTH = 1
DEEPNORM_ALPHA = (2.0 * DEPTH) ** 0.25
LAMBDA_INIT = 0.8 - 0.6 * math.exp(-0.3 * 0)

LANES = 128
BF16_SUBLANES = 16
VMEM_LIMIT = 48 * 1024 * 1024

ADA_COLS = 1024
TM_PROJ = 1024
TQ_MAX = 1024
Q_TILES_MIN = 4
TK = 512
TM_MIX = 1024
TM_COMBINE = 1024
SC_WINDOW = 256
EXPERT_BLOCK = 1024
ROUTER_LANES = 128
EXPERT_LANE0 = N_GROUPS
ROUTE_FIELDS = 8

BF16 = jnp.bfloat16
F32 = jnp.float32


def _cparams(sem):
    return pltpu.CompilerParams(dimension_semantics=sem, vmem_limit_bytes=VMEM_LIMIT)


def _layernorm(x):
    mu = jnp.mean(x, axis=-1, keepdims=True)
    xc = x - mu
    var = jnp.mean(xc * xc, axis=-1, keepdims=True)
    return xc * lax.rsqrt(var + LN_EPS)


def _split_hi_lo(a):
    hi = a.astype(BF16)
    lo = (a - hi.astype(F32)).astype(BF16)
    return hi, lo


def _dot(a, b):
    return jnp.dot(a, b, preferred_element_type=F32)


HALF = D_MODEL // 2
U32 = jnp.uint32
PLANES = HALF // LANES


def _pack_rows(x):
    lo = lax.bitcast_convert_type(x[:, :HALF].astype(BF16).astype(F32), U32)
    hi = lax.bitcast_convert_type(x[:, HALF:].astype(BF16).astype(F32), U32)
    return (lo >> 16) | hi


def _unpack_rows(w):
    lo = lax.bitcast_convert_type(w << 16, F32)
    hi = lax.bitcast_convert_type(w & jnp.uint32(0xFFFF0000), F32)
    return lo, hi


def _ada_kernel(c_ref, w_ref, b_ref, o_ref):
    c = c_ref[...]
    s = c * jax.nn.sigmoid(c)
    s_hi, s_lo = _split_hi_lo(s)
    w_hi, w_lo = _split_hi_lo(w_ref[...])
    acc = _dot(s_hi, w_hi) + _dot(s_lo, w_hi) + _dot(s_hi, w_lo)
    o_ref[...] = acc + b_ref[...]


def _ada(c, w, b):
    bsz = c.shape[0]
    n = w.shape[1]
    tn = ADA_COLS
    return pl.pallas_call(
        _ada_kernel,
        out_shape=jax.ShapeDtypeStruct((bsz, n), F32),
        grid=(n // tn,),
        in_specs=[
            pl.BlockSpec((bsz, D_MODEL), lambda j: (0, 0)),
            pl.BlockSpec((D_MODEL, tn), lambda j: (0, j)),
            pl.BlockSpec((1, tn), lambda j: (0, j)),
        ],
        out_specs=pl.BlockSpec((bsz, tn), lambda j: (0, j)),
        compiler_params=_cparams(("arbitrary",)),
        name="ada",
    )(c, w, b)


def _rope(x, cos_t, sin_t):
    half = QK_HEAD_DIM // 2
    lane = lax.broadcasted_iota(jnp.int32, x.shape, 1)
    upper = (lane & half) != 0
    partner = jnp.where(upper, pltpu.roll(x, half, axis=1), pltpu.roll(x, LANES - half, axis=1))
    return x * cos_t + partner * sin_t


def _in_proj_kernel(x_ref, ada_ref, w_ref, cos_ref, sin_ref,
                    q_ref, k_ref, v_ref, cb_ref, u_ref):
    ada = ada_ref[0]
    sh_m = ada[ADA_SH_M:ADA_SH_M + 1, :]
    sc_m = ada[ADA_SC_M:ADA_SC_M + 1, :]
    qk_scale = QK_HEAD_DIM ** -0.5 * math.log2(math.e)
    hm = TM_PROJ // 2
    halves = [slice(h * hm, (h + 1) * hm) for h in range(2)]
    hs = [(_layernorm(x_ref[rows, :]) * (1.0 + sc_m) + sh_m).astype(BF16) for rows in halves]
    for rows, h in zip(halves, hs):
        cos_t = cos_ref[rows, :]
        sin_t = sin_ref[rows, :]
        q = _dot(h, w_ref[:, COL_Q:COL_K])
        k = _dot(h, w_ref[:, COL_K:COL_V])
        for j in range(Q_WIDTH // LANES):
            lo = j * LANES
            q_ref[rows, lo:lo + LANES] = (_rope(q[:, lo:lo + LANES], cos_t, sin_t) * qk_scale).astype(BF16)
            k_ref[rows, lo:lo + LANES] = _rope(k[:, lo:lo + LANES], cos_t, sin_t).astype(BF16)
        v_ref[rows, :] = _dot(h, w_ref[:, COL_V:COL_B]).astype(BF16)
        cb_ref[rows, :] = _dot(h, w_ref[:, COL_B:COL_C]).astype(BF16)
        cc = _dot(h, w_ref[:, COL_C:COL_H])
        ch = _dot(h, w_ref[:, COL_H:COL_END])
        u_ref[rows, :] = (cc * ch).astype(BF16)


def _in_proj(x2d, ada3, w_in_bf, cos_t, sin_t, seq):
    t = x2d.shape[0]
    tm = TM_PROJ
    tiles_per_seq = seq // tm
    out = jax.ShapeDtypeStruct((t, Q_WIDTH), BF16)
    ospec = pl.BlockSpec((tm, Q_WIDTH), lambda i: (i, 0))
    return pl.pallas_call(
        _in_proj_kernel,
        out_shape=(out,) * 5,
        grid=(t // tm,),
        in_specs=[
            pl.BlockSpec((tm, D_MODEL), lambda i: (i, 0)),
            pl.BlockSpec((1, 6, D_MODEL), lambda i: (i // tiles_per_seq, 0, 0)),
            pl.BlockSpec((D_MODEL, IN_WIDTH), lambda i: (0, 0)),
            pl.BlockSpec((tm, LANES), lambda i: (i % tiles_per_seq, 0)),
            pl.BlockSpec((tm, LANES), lambda i: (i % tiles_per_seq, 0)),
        ],
        out_specs=(ospec,) * 5,
        compiler_params=_cparams(("arbitrary",)),
        name="in_proj",
    )(x2d, ada3, w_in_bf, cos_t, sin_t)


def _diff_attn_kernel(q_ref, k_ref, v_ref, lq1_ref, lk1_ref, lq2_ref, lk2_ref, g_ref, o_ref,
                      qq_ref, vx_ref, s_buf, m_ref, acc_ref, *, seq, tq, tk):
    nq = seq // tq
    nk = seq // tk
    n_lane_blocks = tk // LANES

    vx_ref[:, 0:V_HEAD_DIM] = v_ref[0]
    vx_ref[:, V_HEAD_DIM:2 * V_HEAD_DIM] = jnp.ones((seq, V_HEAD_DIM), BF16)
    lam = (jnp.exp(jnp.sum(lq1_ref[...] * lk1_ref[...], axis=-1, keepdims=True))
           - jnp.exp(jnp.sum(lq2_ref[...] * lk2_ref[...], axis=-1, keepdims=True)) + LAMBDA_INIT)

    def load_q(qt, qslot):
        q = q_ref[0, pl.ds(pl.multiple_of(qt * tq, tq), tq), :]
        lane = lax.broadcasted_iota(jnp.int32, q.shape, 1)
        zero = jnp.zeros_like(q)
        qq_ref[qslot, 0:tq, :] = jnp.where(lane < QK_HEAD_DIM, q, zero)
        qq_ref[qslot, tq:2 * tq, :] = jnp.where(lane >= QK_HEAD_DIM, q, zero)

    def scores(qslot, j, slot):
        start = pl.multiple_of(j * tk, tk)
        kc = k_ref[0, pl.ds(start, tk), :]
        s_buf[slot] = lax.dot_general(qq_ref[qslot], kc, (((1,), (1,)), ((), ())), preferred_element_type=F32)

    def softmax_pv(j, slot):
        blocks = [s_buf[slot, :, c * LANES:(c + 1) * LANES] for c in range(n_lane_blocks)]
        mb = blocks[0]
        for c in range(1, n_lane_blocks):
            mb = jnp.maximum(mb, blocks[c])
        m_old = m_ref[...]
        m_new = jnp.maximum(m_old, jnp.max(mb, axis=-1, keepdims=True))
        alpha = jnp.exp2(m_old - m_new)
        m_ref[...] = m_new
        p = jnp.concatenate([jnp.exp2(blk - m_new).astype(BF16) for blk in blocks], axis=1)
        start = pl.multiple_of(j * tk, tk)
        pv = _dot(p, vx_ref[pl.ds(start, tk), :])
        acc_ref[:, 0:V_HEAD_DIM] = alpha * acc_ref[:, 0:V_HEAD_DIM] + pv[:, 0:V_HEAD_DIM]
        acc_ref[:, V_HEAD_DIM:] = alpha * acc_ref[:, V_HEAD_DIM:] + pv[:, V_HEAD_DIM:]

    def q_tile(qt, qslot, has_next):
        m_ref[...] = jnp.full(m_ref.shape, -jnp.inf, F32)
        acc_ref[...] = jnp.zeros(acc_ref.shape, F32)

        def pair(jj, c):
            j = 2 * jj
            scores(qslot, j + 1, 1)
            softmax_pv(j, 0)
            scores(qslot, j + 2, 0)
            softmax_pv(j + 1, 1)
            return c

        lax.fori_loop(0, nk // 2 - 1, pair, 0)
        scores(qslot, nk - 1, 1)
        softmax_pv(nk - 2, 0)
        if has_next:
            load_q(qt + 1, 1 - qslot)
            scores(1 - qslot, 0, 0)
        softmax_pv(nk - 1, 1)

        o = acc_ref[:, 0:V_HEAD_DIM] / acc_ref[:, V_HEAD_DIM:]
        of = o[:tq] - lam * o[tq:]
        of = of * lax.rsqrt(jnp.mean(of * of, axis=-1, keepdims=True) + RMS_EPS)
        of = of * g_ref[...] * (1.0 - LAMBDA_INIT)
        o_ref[0, pl.ds(pl.multiple_of(qt * tq, tq), tq), :] = of.astype(BF16)

    load_q(0, 0)
    scores(0, 0, 0)

    def q_pair(i, c):
        q_tile(2 * i, 0, True)
        q_tile(2 * i + 1, 1, True)
        return c

    lax.fori_loop(0, nq // 2 - 1, q_pair, 0)
    q_tile(nq - 2, 0, True)
    q_tile(nq - 1, 1, False)


def _diff_attn(q, k, v, lq1, lk1, lq2, lk2, g):
    b, seq, _ = q.shape
    tk = TK
    tq = min(TQ_MAX, seq // Q_TILES_MIN)
    lam_spec = pl.BlockSpec((1, QK_HEAD_DIM), lambda bi, h: (0, 0))
    head_spec = pl.BlockSpec((1, seq, LANES), lambda bi, h: (bi, 0, h))
    return pl.pallas_call(
        functools.partial(_diff_attn_kernel, seq=seq, tq=tq, tk=tk),
        out_shape=jax.ShapeDtypeStruct((b, seq, ATTN_WIDTH), BF16),
        grid=(b, N_DIFF_HEADS),
        in_specs=[
            head_spec, head_spec, head_spec,
            lam_spec, lam_spec, lam_spec, lam_spec,
            pl.BlockSpec((1, V_HEAD_DIM), lambda bi, h: (0, 0)),
        ],
        out_specs=head_spec,
        scratch_shapes=[
            pltpu.VMEM((2, 2 * tq, LANES), BF16),
            pltpu.VMEM((seq, 2 * V_HEAD_DIM), BF16),
            pltpu.VMEM((2, 2 * tq, tk), F32),
            pltpu.VMEM((2 * tq, LANES), F32),
            pltpu.VMEM((2 * tq, 2 * V_HEAD_DIM), F32),
        ],
        compiler_params=_cparams(("arbitrary", "arbitrary")),
        name="diff_attn",
    )(q, k, v, lq1, lk1, lq2, lk2, g)


def _lane_min_index(mask, lane_f):
    return jnp.min(jnp.where(mask, lane_f, float(ROUTER_LANES)), axis=-1, keepdims=True)


def _mix_route_kernel(x_ref, attn_ref, cb_ref, u_ref, uprev_ref, unext_ref, ada_ref,
                      cw_ref, cbias_ref, woa_ref, woc_ref, g1_ref, b1_ref, wr_ref, br_ref,
                      x1_ref, h2_ref, rt_ref, rtt_ref, cnt_ref, base_ref, *, seq):
    i = pl.program_id(0)
    tm = TM_MIX

    @pl.when(i == 0)
    def _():
        base_ref[...] = jnp.zeros_like(base_ref)

    ada = ada_ref[0]
    g_m = ada[ADA_G_M:ADA_G_M + 1, :]
    sh_f = ada[ADA_SH_F:ADA_SH_F + 1, :]
    sc_f = ada[ADA_SC_F:ADA_SC_F + 1, :]

    u = u_ref[...].astype(F32)
    row = lax.broadcasted_iota(jnp.int32, u.shape, 0)
    not_seq_start = ((i * tm) % seq != 0).astype(F32)
    not_seq_end = (((i + 1) * tm) % seq != 0).astype(F32)
    halo_prev = uprev_ref[...].astype(F32)[BF16_SUBLANES - 1:BF16_SUBLANES, :] * not_seq_start
    halo_next = unext_ref[...].astype(F32)[0:1, :] * not_seq_end
    u_prev = jnp.where(row == 0, halo_prev, pltpu.roll(u, 1, axis=0))
    u_next = jnp.where(row == tm - 1, halo_next, pltpu.roll(u, tm - 1, axis=0))
    cw = cw_ref[...]

    hm = tm // 2
    halves = [slice(h * hm, (h + 1) * hm) for h in range(2)]
    mixes = []
    for rows in halves:
        y = cbias_ref[...] + u_prev[rows, :] * cw[0:1, :]
        y = y + u[rows, :] * cw[1:2, :]
        y = y + u_next[rows, :] * cw[2:3, :]
        conv = (cb_ref[rows, :].astype(F32) * y).astype(BF16)
        mixes.append(_dot(attn_ref[rows, :], woa_ref[...]) + _dot(conv, woc_ref[...]))

    lane = lax.broadcasted_iota(jnp.int32, (hm, ROUTER_LANES), 1)
    lane_f = lane.astype(F32)
    neg = -jnp.inf
    tri = jnp.where(lax.broadcasted_iota(jnp.int32, (hm, hm), 1) < lax.broadcasted_iota(jnp.int32, (hm, hm), 0),
                    1.0, 0.0).astype(BF16)
    base = base_ref[...]
    for rows, mix in zip(halves, mixes):
        x1 = _layernorm(DEEPNORM_ALPHA * x_ref[rows, :] + g_m * mix) * g1_ref[...] + b1_ref[...]
        x1_ref[rows, :] = x1
        h2 = _layernorm(x1) * (1.0 + sc_f) + sh_f
        for c, plane in enumerate(_to_planes(_pack_rows(h2))):
            h2_ref[c, rows, :] = plane

        h_hi, h_lo = _split_hi_lo(h2)
        both = _dot(h_hi, wr_ref[...])
        logits = (both[:, :ROUTER_LANES] + both[:, ROUTER_LANES:]
                  + _dot(h_lo, wr_ref[:, :ROUTER_LANES]) + br_ref[...])

        lg = jnp.where(lane < N_GROUPS, logits, neg)
        lg_max = jnp.max(lg, axis=-1, keepdims=True)
        g_sel = _lane_min_index(lg == lg_max, lane_f)
        pg_sel = 1.0 / jnp.sum(jnp.exp(lg - lg_max), axis=-1, keepdims=True)

        first = EXPERT_LANE0 + EXPERTS_PER_GROUP * g_sel
        in_group = (lane_f >= first) & (lane_f < first + EXPERTS_PER_GROUP)
        le = jnp.where(in_group, logits, neg)
        l0 = jnp.max(le, axis=-1, keepdims=True)
        i0 = _lane_min_index(le == l0, lane_f)
        le2 = jnp.where(lane_f == i0, neg, le)
        l1 = jnp.max(le2, axis=-1, keepdims=True)
        i1 = _lane_min_index(le2 == l1, lane_f)
        t_exp = jnp.exp(l1 - l0)
        p0 = 1.0 / (1.0 + t_exp)
        w0 = pg_sel * p0
        w1 = pg_sel * (t_exp * p0)

        oh0 = lane_f == i0
        oh1 = lane_f == i1
        onehots = jnp.concatenate([jnp.where(oh0, 1.0, 0.0), jnp.where(oh1, 1.0, 0.0)], axis=1).astype(BF16)
        before = _dot(tri, onehots)
        cnt0 = jnp.sum(jnp.where(oh0, 1.0, 0.0), axis=0, keepdims=True)
        cnt1 = jnp.sum(jnp.where(oh1, 1.0, 0.0), axis=0, keepdims=True)
        rank0 = jnp.sum(jnp.where(oh0, before[:, :ROUTER_LANES] + base, 0.0), axis=-1, keepdims=True)
        rank1 = jnp.sum(jnp.where(oh1, before[:, ROUTER_LANES:] + base + cnt0, 0.0), axis=-1, keepdims=True)
        base = base + cnt0 + cnt1

        rt = jnp.where(lane == 0, i0 - EXPERT_LANE0, 0.0)
        rt = jnp.where(lane == 1, i1 - EXPERT_LANE0, rt)
        rt = jnp.where(lane == 2, rank0, rt)
        rt = jnp.where(lane == 3, rank1, rt)
        rt = jnp.where(lane == 4, w0, rt)
        rt = jnp.where(lane == 5, w1, rt)
        rt_ref[rows, :] = rt
        rtt_ref[:, rows] = rt.T[0:ROUTE_FIELDS, :]
    base_ref[...] = base
    cnt_ref[...] = base


def _mix_route(x2d, attn2d, cb, u, ada3, conv_w, conv_b, woa, woc, g1, b1, wr, br, seq):
    t = x2d.shape[0]
    tm = TM_MIX
    tiles_per_seq = seq // tm
    hb = tm // BF16_SUBLANES
    n_halo = t // BF16_SUBLANES
    const = lambda i: (0, 0)
    return pl.pallas_call(
        functools.partial(_mix_route_kernel, seq=seq),
        out_shape=(
            jax.ShapeDtypeStruct((t, D_MODEL), F32),
            jax.ShapeDtypeStruct((PLANES, t, LANES), U32),
            jax.ShapeDtypeStruct((t, ROUTER_LANES), F32),
            jax.ShapeDtypeStruct((ROUTE_FIELDS, t), F32),
            jax.ShapeDtypeStruct((1, ROUTER_LANES), F32),
        ),
        grid=(t // tm,),
        in_specs=[
            pl.BlockSpec((tm, D_MODEL), lambda i: (i, 0)),
            pl.BlockSpec((tm, ATTN_WIDTH), lambda i: (i, 0)),
            pl.BlockSpec((tm, CONV_WIDTH), lambda i: (i, 0)),
            pl.BlockSpec((tm, CONV_WIDTH), lambda i: (i, 0)),
            pl.BlockSpec((BF16_SUBLANES, CONV_WIDTH), lambda i: (jnp.maximum(i * hb - 1, 0), 0)),
            pl.BlockSpec((BF16_SUBLANES, CONV_WIDTH), lambda i: (jnp.minimum((i + 1) * hb, n_halo - 1), 0)),
            pl.BlockSpec((1, 6, D_MODEL), lambda i: (i // tiles_per_seq, 0, 0)),
            pl.BlockSpec((CONV_K, CONV_WIDTH), const),
            pl.BlockSpec((1, CONV_WIDTH), const),
            pl.BlockSpec((ATTN_WIDTH, D_MODEL), const),
            pl.BlockSpec((CONV_WIDTH, D_MODEL), const),
            pl.BlockSpec((1, D_MODEL), const),
            pl.BlockSpec((1, D_MODEL), const),
            pl.BlockSpec((D_MODEL, 2 * ROUTER_LANES), const),
            pl.BlockSpec((1, ROUTER_LANES), const),
        ],
        out_specs=(
            pl.BlockSpec((tm, D_MODEL), lambda i: (i, 0)),
            pl.BlockSpec((PLANES, tm, LANES), lambda i: (0, i, 0)),
            pl.BlockSpec((tm, ROUTER_LANES), lambda i: (i, 0)),
            pl.BlockSpec((ROUTE_FIELDS, tm), lambda i: (0, i)),
            pl.BlockSpec((1, ROUTER_LANES), const),
        ),
        scratch_shapes=[pltpu.VMEM((1, ROUTER_LANES), F32)],
        compiler_params=_cparams(("arbitrary",)),
        name="mix_route",
    )(x2d, attn2d, cb, u, u, u, ada3, conv_w, conv_b, woa, woc, g1, b1, wr, br)


def _sc_mesh():
    return plsc.VectorSubcoreMesh(core_axis_name="core", subcore_axis_name="subcore")


def _sc_scatter_rows(src, idx, n_dst):
    m = idx.shape[0]
    w = SC_WINDOW
    n_src_blocks = src.shape[0] // w

    @pl.kernel(out_type=jax.ShapeDtypeStruct((n_dst, LANES), src.dtype), mesh=_sc_mesh(), scratch_types=[])
    def scatter(x_hbm, i_hbm, o_hbm):
        def body(x_vmem, i_vmem):
            pltpu.sync_copy(x_vmem, o_hbm.at[i_vmem.at[0]])

        pltpu.emit_pipeline(
            body, grid=(m // w,),
            in_specs=[pl.BlockSpec((w, LANES), index_map=lambda i: (i % n_src_blocks, 0)),
                      pl.BlockSpec((1, w), index_map=lambda i: (0, i))],
            out_specs=[],
            core_axis_name=("core", "subcore"),
            dimension_semantics=(pltpu.PARALLEL,),
        )(x_hbm, i_hbm)

    return scatter(src, idx.reshape(1, m))


def _sc_gather_rows(src, idx):
    m = idx.shape[0]
    w = SC_WINDOW

    @pl.kernel(out_type=jax.ShapeDtypeStruct((m, LANES), src.dtype), mesh=_sc_mesh(), scratch_types=[])
    def gather(x_hbm, i_hbm, o_hbm):
        def body(i_vmem, o_vmem):
            pltpu.sync_copy(x_hbm.at[i_vmem.at[0]], o_vmem)

        pltpu.emit_pipeline(
            body, grid=(m // w,),
            in_specs=[pl.BlockSpec((1, w), index_map=lambda i: (0, i))],
            out_specs=[pl.BlockSpec((w, LANES), index_map=lambda i: (i, 0))],
            core_axis_name=("core", "subcore"),
            dimension_semantics=(pltpu.PARALLEL,),
        )(i_hbm, o_hbm)

    return gather(src, idx.reshape(1, m))


def _to_planes(words):
    return [words[:, c * LANES:(c + 1) * LANES] for c in range(PLANES)]


def _experts_kernel(blk_exp_ref, n_used_ref, xs_ref, wg_ref, wu_ref, wd_ref, y_ref):
    del blk_exp_ref

    @pl.when(pl.program_id(0) < n_used_ref[0])
    def _():
        hb = EXPERT_BLOCK // 2
        halves = [slice(h * hb, (h + 1) * hb) for h in range(2)]
        wg_lo, wg_hi = wg_ref[0, :HALF, :].astype(BF16), wg_ref[0, HALF:, :].astype(BF16)
        wu_lo, wu_hi = wu_ref[0, :HALF, :].astype(BF16), wu_ref[0, HALF:, :].astype(BF16)
        wd = wd_ref[0].astype(BF16)
        gated = []
        for rows in halves:
            x_lo, x_hi = _unpack_rows(jnp.concatenate([xs_ref[c, rows, :] for c in range(PLANES)], axis=1))
            x_lo = x_lo.astype(BF16)
            x_hi = x_hi.astype(BF16)
            gated.append((_dot(x_lo, wg_lo) + _dot(x_hi, wg_hi), _dot(x_lo, wu_lo) + _dot(x_hi, wu_hi)))
        for rows, (g, up) in zip(halves, gated):
            act = (g * jax.nn.sigmoid(g) * up).astype(BF16)
            for c, plane in enumerate(_to_planes(_pack_rows(_dot(act, wd)))):
                y_ref[c, rows, :] = plane


def _experts(blk_exp, n_used, xs, wg, wu, wd):
    rows = xs.shape[1]
    blk = EXPERT_BLOCK

    def last_used(i, nu):
        return jnp.minimum(i, nu[0] - 1)

    return pl.pallas_call(
        _experts_kernel,
        out_shape=jax.ShapeDtypeStruct((PLANES, rows, LANES), U32),
        grid_spec=pltpu.PrefetchScalarGridSpec(
            num_scalar_prefetch=2,
            grid=(rows // blk,),
            in_specs=[
                pl.BlockSpec((PLANES, blk, LANES), lambda i, be, nu: (0, last_used(i, nu), 0)),
                pl.BlockSpec((1, D_MODEL, D_FF_EXPERT), lambda i, be, nu: (be[last_used(i, nu)], 0, 0)),
                pl.BlockSpec((1, D_MODEL, D_FF_EXPERT), lambda i, be, nu: (be[last_used(i, nu)], 0, 0)),
                pl.BlockSpec((1, D_FF_EXPERT, D_MODEL), lambda i, be, nu: (be[last_used(i, nu)], 0, 0)),
            ],
            out_specs=pl.BlockSpec((PLANES, blk, LANES), lambda i, be, nu: (0, last_used(i, nu), 0)),
        ),
        compiler_params=_cparams(("arbitrary",)),
        name="experts",
    )(blk_exp, n_used, xs, wg, wu, wd)


def _combine_kernel(x1_ref, rt_ref, ada_ref, g2_ref, b2_ref, yg_ref, o_ref):
    g_f = ada_ref[0][ADA_G_F:ADA_G_F + 1, :]
    rt = rt_ref[...]
    w0 = rt[:, 4:5]
    w1 = rt[:, 5:6]
    y0_lo, y0_hi = _unpack_rows(jnp.concatenate([yg_ref[0, c] for c in range(PLANES)], axis=1))
    y1_lo, y1_hi = _unpack_rows(jnp.concatenate([yg_ref[1, c] for c in range(PLANES)], axis=1))
    f = jnp.concatenate([y0_lo * w0 + y1_lo * w1, y0_hi * w0 + y1_hi * w1], axis=1)
    z = DEEPNORM_ALPHA * x1_ref[...] + g_f * f
    o_ref[...] = _layernorm(z) * g2_ref[...] + b2_ref[...]


def _combine(x1, rt, ada3, g2, b2, yg, seq):
    t = x1.shape[0]
    tm = TM_COMBINE
    tiles_per_seq = seq // tm
    const = lambda i: (0, 0)
    return pl.pallas_call(
        _combine_kernel,
        out_shape=jax.ShapeDtypeStruct((t, D_MODEL), F32),
        grid=(t // tm,),
        in_specs=[
            pl.BlockSpec((tm, D_MODEL), lambda i: (i, 0)),
            pl.BlockSpec((tm, ROUTER_LANES), lambda i: (i, 0)),
            pl.BlockSpec((1, 6, D_MODEL), lambda i: (i // tiles_per_seq, 0, 0)),
            pl.BlockSpec((1, D_MODEL), const),
            pl.BlockSpec((1, D_MODEL), const),
            pl.BlockSpec((2, PLANES, tm, LANES), lambda i: (0, 0, i, 0)),
        ],
        out_specs=pl.BlockSpec((tm, D_MODEL), lambda i: (i, 0)),
        compiler_params=_cparams(("arbitrary",)),
        name="combine",
    )(x1, rt, ada3, g2, b2, yg)


def _rope_tables(seq):
    half = QK_HEAD_DIM // 2
    inv = 1.0 / (ROPE_THETA ** (jnp.arange(0, QK_HEAD_DIM, 2, dtype=F32) / QK_HEAD_DIM))
    ang = jnp.arange(seq, dtype=F32)[:, None] * inv[None, :]
    cos, sin = jnp.cos(ang), jnp.sin(ang)
    cos_t = jnp.tile(cos, (1, LANES // half))
    sin_t = jnp.tile(jnp.concatenate([-sin, sin], axis=1), (1, LANES // QK_HEAD_DIM))
    return cos_t, sin_t


def _routing_tables(rt_t, counts, n_rows_max):
    t = rt_t.shape[1]
    blk = EXPERT_BLOCK
    counts = counts[0, EXPERT_LANE0:EXPERT_LANE0 + N_EXPERTS].astype(jnp.int32)
    padded = ((counts + blk - 1) // blk) * blk
    pend = jnp.cumsum(padded)
    pstart = pend - padded
    e = rt_t[0:2].astype(jnp.int32)
    rank = rt_t[2:4].astype(jnp.int32)
    first_row = jnp.zeros_like(e)
    for j in range(N_EXPERTS):
        first_row = jnp.where(e == j, pstart[j], first_row)
    dest = first_row + rank
    plane0 = (jnp.arange(PLANES, dtype=jnp.int32) * n_rows_max)[None, :, None]
    row_idx = (dest[:, None, :] + plane0).reshape(2 * PLANES * t)
    blk_start = jnp.arange(n_rows_max // blk, dtype=jnp.int32) * blk
    blk_exp = jnp.minimum(jnp.sum(blk_start[:, None] >= pend[None, :], axis=1), N_EXPERTS - 1).astype(jnp.int32)
    n_used = (pend[-1:] // blk).astype(jnp.int32)
    return row_idx, blk_exp, n_used


def _encoder(x, ada, p):
    b, seq, d = x.shape
    t = b * seq
    x2d = x.reshape(t, d)
    ada3 = ada.reshape(b, 6, d)
    cos_t, sin_t = _rope_tables(seq)
    q, k, v, cb, u = _in_proj(x2d, ada3, p["w_in"], cos_t, sin_t, seq)
    attn = _diff_attn(q.reshape(b, seq, Q_WIDTH), k.reshape(b, seq, Q_WIDTH), v.reshape(b, seq, ATTN_WIDTH),
                      p["lq1"], p["lk1"], p["lq2"], p["lk2"], p["subln_g"])
    x1, h2, rt, rt_t, counts = _mix_route(x2d, attn.reshape(t, ATTN_WIDTH), cb, u, ada3, p["conv_w"], p["conv_b"],
                                    p["woa"], p["woc"], p["ln1_g"], p["ln1_b"], p["wr"], p["br"], seq)
    n_rows_max = (2 * t // EXPERT_BLOCK + N_EXPERTS) * EXPERT_BLOCK
    row_idx, blk_exp, n_used = _routing_tables(rt_t, counts, n_rows_max)
    xs = _sc_scatter_rows(h2.reshape(PLANES * t, LANES), row_idx, PLANES * n_rows_max)
    y = _experts(blk_exp, n_used, xs.reshape(PLANES, n_rows_max, LANES), p["wg"], p["wu"], p["wd"])
    yg = _sc_gather_rows(y.reshape(PLANES * n_rows_max, LANES), row_idx)
    out = _combine(x1, rt, ada3, p["ln2_g"], p["ln2_b"], yg.reshape(2, PLANES, t, LANES), seq)
    return out.reshape(b, seq, d)


def kernel(x_prompt, x_sample, c_prompt, c_sample, w_ada, b_ada, w_in, lambda_q1, lambda_k1, lambda_q2, lambda_k2, attn_subln_g, conv_w, conv_b, w_out, ln1_g, ln1_b, router_group_w, router_group_b, router_expert_w, router_expert_b, expert_w_gate, expert_w_up, expert_w_down, ln2_g, ln2_b):
    l = 0
    wr = jnp.concatenate([router_group_w[l], router_expert_w[l]], axis=1)
    wr = jnp.pad(wr, ((0, 0), (0, ROUTER_LANES - wr.shape[1])))
    wr_hi, wr_lo = _split_hi_lo(wr)
    br = jnp.concatenate([router_group_b[l], router_expert_b[l]])
    br = jnp.pad(br, (0, ROUTER_LANES - br.shape[0])).reshape(1, ROUTER_LANES)
    w_out_bf = w_out[l].astype(BF16)
    p = {
        "w_in": w_in[l].astype(BF16),
        "lq1": lambda_q1[l].reshape(1, -1), "lk1": lambda_k1[l].reshape(1, -1),
        "lq2": lambda_q2[l].reshape(1, -1), "lk2": lambda_k2[l].reshape(1, -1),
        "subln_g": attn_subln_g[l].reshape(1, -1),
        "conv_w": conv_w[l], "conv_b": conv_b[l].reshape(1, -1),
        "woa": w_out_bf[:ATTN_WIDTH], "woc": w_out_bf[ATTN_WIDTH:],
        "ln1_g": ln1_g[l].reshape(1, -1), "ln1_b": ln1_b[l].reshape(1, -1),
        "wr": jnp.concatenate([wr_hi, wr_lo], axis=1), "br": br,
        "wg": expert_w_gate[l], "wu": expert_w_up[l], "wd": expert_w_down[l],
        "ln2_g": ln2_g[l].reshape(1, -1), "ln2_b": ln2_b[l].reshape(1, -1),
    }
    ada = _ada(jnp.concatenate([c_prompt, c_sample], axis=0), w_ada[l], b_ada[l].reshape(1, -1))
    n_prompt = c_prompt.shape[0]
    y_prompt = _encoder(x_prompt, ada[:n_prompt], p)
    y_sample = _encoder(x_sample, ada[n_prompt:], p)
    return (y_prompt, y_sample)
```
